```python
import math
import jax
import jax.numpy as jnp
from jax import lax
import numpy as np

D_MODEL = 1024
BATCH = 32
SEQ = 256
DEPTH = 4
DEC_BATCH = 4
DEC_SEQ = 4096
PAST_LEN = 512

GRID_W = 64
MLA_H = 8
MLA_DN = 64
MLA_DR = 32
MLA_DV = 64
Q_RANK = 256
KV_RANK = 128
ML_H = 4
ML_D = 64
ML_W = 256
HG_H = 4
HG_DK = 64
HG_DV = 64
HG_W = 256
HG_KW = 256
MIX_W = 1024
FF = 2816
CHUNK = 64
Q_BLOCK = 128
ROPE_BASE = 10000.0
ALPHA = (2 * DEPTH) ** 0.25
BETA = (8 * DEPTH) ** -0.25
EPS = 1e-6
MASK_NEG = -1e30
TINY = 1e-30
IN_SIZES = (Q_RANK, KV_RANK, MLA_DR, ML_W, ML_W, ML_W, ML_W, ML_H, ML_H, ML_H, ML_H, HG_W, HG_KW, HG_KW, HG_W, HG_W)
IN_COLS = 2736

kernel_name = 'hybrid_mla_mlstm_hgrn2_diffusion_step'


def layernorm(x, g, b):
    xf = x.astype(jnp.float32)
    mu = xf.mean(-1, keepdims=True)
    var = jnp.mean(jnp.square(xf - mu), -1, keepdims=True)
    return ((xf - mu) * lax.rsqrt(var + EPS) * g.astype(jnp.float32) + b.astype(jnp.float32)).astype(x.dtype)


def rmsnorm(x, g):
    xf = x.astype(jnp.float32)
    return (xf * lax.rsqrt(jnp.mean(xf * xf, -1, keepdims=True) + EPS) * g.astype(jnp.float32)).astype(x.dtype)


def to_heads(x, n_heads):
    B, T, W = x.shape
    return x.reshape(B, T, n_heads, W // n_heads).transpose(0, 2, 1, 3)


def from_heads(x):
    B, H, T, d = x.shape
    return x.transpose(0, 2, 1, 3).reshape(B, T, H * d)


def flip(a):
    return jnp.flip(a, axis=2)


def axial_rope(n):
    n_rows = n // GRID_W
    row = jnp.repeat(jnp.arange(n_rows, dtype=jnp.float32), GRID_W)
    col = jnp.tile(jnp.arange(GRID_W, dtype=jnp.float32), n_rows)
    half = MLA_DR // 2
    freqs = ROPE_BASE ** (-jnp.arange(half // 2, dtype=jnp.float32) * (2.0 / half))
    ar = row[:, None] * freqs
    ac = col[:, None] * freqs
    ang = jnp.concatenate([ar, ar, ac, ac], -1)
    return jnp.cos(ang), jnp.sin(ang)


def apply_rope(x, cos, sin):
    x1, x2, x3, x4 = jnp.split(x, 4, axis=-1)
    rot = jnp.concatenate([-x2, x1, -x4, x3], -1)
    return (x.astype(jnp.float32) * cos + rot.astype(jnp.float32) * sin).astype(x.dtype)


def attend(q_nope, q_pe, k_nope, k_pe, v):
    B, T, H, _ = q_nope.shape
    nb = T // Q_BLOCK
    scale = (MLA_DN + MLA_DR) ** -0.5

    def blocks(a):
        return jnp.moveaxis(a.reshape((B, nb, Q_BLOCK) + a.shape[2:]), 1, 0)

    def one_block(qs):
        qn, qp = qs
        s = (jnp.einsum('bqhd,bkhd->bhqk', qn, k_nope).astype(jnp.float32)
             + jnp.einsum('bqhr,bkr->bhqk', qp, k_pe).astype(jnp.float32))
        p = jax.nn.softmax(s * scale, axis=-1).astype(v.dtype)
        return jnp.einsum('bhqk,bkhd->bqhd', p, v)

    o = lax.map(one_block, (blocks(q_nope), blocks(q_pe)))
    return jnp.moveaxis(o, 0, 1).reshape(B, T, H * MLA_DV)


def mlstm_scan(q, k, v, ig, lf, C0, n0, m0):
    B, H, T, d = q.shape
    nc = T // CHUNK

    def chunks(a):
        return jnp.moveaxis(a.reshape(a.shape[:2] + (nc, CHUNK) + a.shape[3:]), 2, 0)

    causal = jnp.tril(jnp.ones((CHUNK, CHUNK), dtype=bool))

    def step(carry, inp):
        C, n, m = carry
        qc, kc, vc, ic, fc = inp
        b = jnp.cumsum(fc, axis=-1)
        a = b + m[..., None]
        Dm = jnp.where(causal, b[..., :, None] - b[..., None, :] + ic[..., None, :], MASK_NEG)
        mt = jnp.maximum(a, Dm.max(-1))
        ws = jnp.exp(a - mt)
        qk = jnp.einsum('bhtd,bhsd->bhts', qc, kc) * jnp.exp(Dm - mt[..., None])
        num = ws[..., None] * jnp.einsum('bhtd,bhde->bhte', qc, C) + jnp.einsum('bhts,bhse->bhte', qk, vc)
        den = ws * jnp.einsum('bhtd,bhd->bht', qc, n) + qk.sum(-1)
        h = num / jnp.maximum(jnp.abs(den), jnp.exp(-mt))[..., None]
        g = b[..., -1]
        u = g[..., None] - b + ic
        m_new = jnp.maximum(g + m, u.max(-1))
        sc = jnp.exp(g + m - m_new)
        ku = kc * jnp.exp(u - m_new[..., None])[..., None]
        C_new = sc[..., None, None] * C + jnp.einsum('bhsd,bhse->bhde', ku, vc)
        n_new = sc[..., None] * n + ku.sum(2)
        return (C_new, n_new, m_new), h

    (C, n, m), hs = lax.scan(step, (C0, n0, m0), tuple(chunks(a) for a in (q, k, v, ig, lf)))
    return jnp.moveaxis(hs, 0, 2).reshape(B, H, T, d), C, n, m


def hgrn_scan(q, k, v, lg, S0):
    B, H, T, dk = q.shape
    dv = v.shape[-1]
    nc = T // CHUNK

    def chunks(a):
        return jnp.moveaxis(a.reshape(a.shape[:2] + (nc, CHUNK) + a.shape[3:]), 2, 0)

    causal = jnp.tril(jnp.ones((CHUNK, CHUNK), dtype=bool))[:, :, None]

    def step(S, inp):
        qc, kc, vc, gc = inp
        Bc = jnp.cumsum(gc, axis=2)
        inter = jnp.einsum('bhtc,bhce->bhte', qc * jnp.exp(Bc), S)
        diff = Bc[:, :, :, None, :] - Bc[:, :, None, :, :]
        decay = jnp.where(causal, jnp.exp(jnp.where(causal, diff, 0.0)), 0.0)
        A = jnp.einsum('bhtsc,bhsc->bhts', qc[:, :, :, None, :] * decay, kc)
        o = inter + jnp.einsum('bhts,bhse->bhte', A, vc)
        gL = Bc[:, :, -1:, :]
        S_new = jnp.exp(gL[:, :, 0, :])[..., None] * S + jnp.einsum('bhsc,bhse->bhce', kc * jnp.exp(gL - Bc), vc)
        return S_new, o

    S, os_ = lax.scan(step, S0, tuple(chunks(a) for a in (q, k, v, lg)))
    return jnp.moveaxis(os_, 0, 2).reshape(B, H, T, dv), S


def mla_expand(ckv_n, w_ukv):
    B, T, _ = ckv_n.shape
    kv = (ckv_n @ w_ukv).reshape(B, T, MLA_H, MLA_DN + MLA_DV)
    return kv[..., :MLA_DN], kv[..., MLA_DN:]


def mla_mixer(cq, ckv, kpe, q_norm, w_uq, kv_norm, w_ukv, cache, rope):
    B, T, _ = cq.shape
    q = (rmsnorm(cq, q_norm) @ w_uq).reshape(B, T, MLA_H, MLA_DN + MLA_DR)
    q_nope, q_pe = q[..., :MLA_DN], q[..., MLA_DN:]
    ckv_n = rmsnorm(ckv, kv_norm)
    k_nope, v = mla_expand(ckv_n, w_ukv)
    if cache is None:
        return attend(q_nope, q_pe, k_nope, kpe, v), (ckv_n, kpe)
    cos, sin = rope
    q_pe = apply_rope(q_pe, cos[:, None, :], sin[:, None, :])
    kpe = apply_rope(kpe, cos, sin)
    ckv_c, kpe_c = cache
    kn_c, v_c = mla_expand(ckv_c, w_ukv)
    out = attend(q_nope, q_pe,
                 jnp.concatenate([k_nope, kn_c], axis=1),
                 jnp.concatenate([kpe, kpe_c], axis=1),
                 jnp.concatenate([v, v_c], axis=1))
    return out, None


def mlstm_mixer(mq, mk, mv, mo, mi_f, mi_b, mf_f, mf_b, norm_g, init):
    f32 = jnp.float32
    q = to_heads(mq, ML_H).astype(f32)
    k = to_heads(mk, ML_H).astype(f32) * (ML_D ** -0.5)
    v = to_heads(mv, ML_H).astype(f32)

    def gate(a):
        return a.astype(f32).transpose(0, 2, 1)

    ig_f, ig_b = gate(mi_f), gate(mi_b)
    lf_f, lf_b = jax.nn.log_sigmoid(gate(mf_f)), jax.nn.log_sigmoid(gate(mf_b))
    (Cf0, nf0, mf0), (Cb0, nb0, mb0) = init
    hf, Cf, nf, mf = mlstm_scan(q, k, v, ig_f, lf_f, Cf0.astype(f32), nf0.astype(f32), mf0.astype(f32))
    hb, Cb, nb, mb = mlstm_scan(flip(q), flip(k), flip(v), flip(ig_b), flip(lf_b),
                                Cb0.astype(f32), nb0.astype(f32), mb0.astype(f32))
    h = hf + flip(hb)
    mu = h.mean(-1, keepdims=True)
    var = jnp.mean(jnp.square(h - mu), -1, keepdims=True)
    h = (h - mu) * lax.rsqrt(var + EPS)
    out = from_heads(h) * norm_g.astype(f32) * jax.nn.sigmoid(mo.astype(f32))
    states = (jnp.stack([Cf, Cb], 1), jnp.stack([nf, nb], 1), jnp.stack([mf, mb], 1))
    return out.astype(mq.dtype), states


def hgrn_mixer(gq, gf_f, gf_b, gi, gg, lb, norm_g, init):
    f32 = jnp.float32
    q = jax.nn.silu(to_heads(gq, HG_H).astype(f32))
    v = to_heads(gi, HG_H).astype(f32)
    lb = lb.astype(f32).reshape(1, HG_H, 1, HG_DK)

    def decay(fr):
        fr = to_heads(fr, HG_H).astype(f32)
        f = lb + (1.0 - lb) * jax.nn.sigmoid(fr)
        return (1.0 - lb) * jax.nn.sigmoid(-fr), jnp.log(jnp.maximum(f, TINY))

    kf, lgf = decay(gf_f)
    kb, lgb = decay(gf_b)
    S0f, S0b = init
    of, Sf = hgrn_scan(q, kf, v, lgf, S0f.astype(f32))
    ob, Sb = hgrn_scan(flip(q), flip(kb), flip(v), flip(lgb), S0b.astype(f32))
    o = of + flip(ob)
    o = o * lax.rsqrt(jnp.mean(o * o, -1, keepdims=True) + EPS)
    out = from_heads(o) * norm_g.astype(f32) * jax.nn.silu(gg.astype(f32))
    return out.astype(gq.dtype), jnp.stack([Sf, Sb], 1)


def trunk_layer(x, cvec, lp, lb, cache, rope):
    B, T, _ = x.shape
    mod = (jax.nn.silu(cvec) @ lp['w_mod'] + lp['b_mod'])[:, None, :]
    sh1, sc1, g1, sh2, sc2, g2 = jnp.split(mod, 6, axis=-1)
    h = x * (1 + sc1) + sh1
    offsets = [int(o) for o in np.cumsum(IN_SIZES)[:-1]]
    (cq, ckv, kpe, mq, mk, mv, mo, mi_f, mi_b, mf_f, mf_b,
     gq, gf_f, gf_b, gi, gg) = jnp.split(h @ lp['w_in'] + lp['b_in'], offsets, axis=-1)
    if cache is None:
        f32 = jnp.float32
        ml0 = (jnp.zeros((B, ML_H, ML_D, ML_D), f32), jnp.zeros((B, ML_H, ML_D), f32), jnp.zeros((B, ML_H), f32))
        ml_init = (ml0, ml0)
        hg0 = jnp.zeros((B, HG_H, HG_DK, HG_DV), f32)
        hg_init = (hg0, hg0)
        mla_cache = None
    else:
        mla_cache, ml_init, hg_init = cache
    a_out, mla_st = mla_mixer(cq, ckv, kpe, lp['mla_q_norm'], lp['w_uq'], lp['mla_kv_norm'], lp['w_ukv'], mla_cache, rope)
    m_out, ml_st = mlstm_mixer(mq, mk, mv, mo, mi_f, mi_b, mf_f, mf_b, lp['mlstm_norm'], ml_init)
    g_out, hg_st = hgrn_mixer(gq, gf_f, gf_b, gi, gg, lb, lp['hgrn_norm'], hg_init)
    mix = jnp.concatenate([a_out, m_out, g_out], axis=-1) @ lp['w_out']
    x = layernorm(ALPHA * x + g1 * mix, lp['ln1_g'], lp['ln1_b'])
    h2 = x * (1 + sc2) + sh2
    gate, up = jnp.split(h2 @ lp['w_ffn_in'], 2, axis=-1)
    x = layernorm(ALPHA * x + g2 * ((jax.nn.silu(gate) * up) @ lp['w_ffn_out']), lp['ln2_g'], lp['ln2_b'])
    return x, (mla_st, ml_st, hg_st)


def setup_inputs(seed: int = 0) -> dict:
    key = jax.random.key(seed)
    ks = jax.random.split(key, 32)
    D = D_MODEL

    def nrm(k, shape, scale=1.0):
        return jax.random.normal(k, shape, jnp.float32) * scale

    b_mod = nrm(ks[11], (DEPTH, 6 * D), 0.02)
    b_mod = b_mod.at[:, 2 * D:3 * D].add(1.0).at[:, 5 * D:6 * D].add(1.0)
    off = np.cumsum((0,) + IN_SIZES)
    b_in = nrm(ks[13], (DEPTH, IN_COLS), 0.02)
    b_in = b_in.at[:, int(off[9]):int(off[11])].add(jnp.tile(jnp.linspace(3.0, 6.0, ML_H), 2))
    return {
        'x_prompt': nrm(ks[0], (BATCH, SEQ, D)),
        'x_sample': nrm(ks[1], (DEC_BATCH, DEC_SEQ, D)),
        'cache_mla_ckv': nrm(ks[2], (DEC_BATCH, DEPTH, PAST_LEN, KV_RANK)),
        'cache_mla_kpe': nrm(ks[3], (DEC_BATCH, DEPTH, PAST_LEN, MLA_DR)),
        'state_mlstm_C': nrm(ks[4], (DEC_BATCH, DEPTH, 2, ML_H, ML_D, ML_D), 0.3),
        'state_mlstm_n': nrm(ks[5], (DEC_BATCH, DEPTH, 2, ML_H, ML_D), 0.3),
        'state_mlstm_m': nrm(ks[6], (DEC_BATCH, DEPTH, 2, ML_H)),
        'state_hgrn_S': nrm(ks[7], (DEC_BATCH, DEPTH, 2, HG_H, HG_DK, HG_DV), 0.5),
        'c': nrm(ks[8], (DEC_BATCH, D)),
        'c_ctx': nrm(ks[9], (D,)),
        'w_mod': nrm(ks[10], (DEPTH, D, 6 * D), 0.5 * D ** -0.5),
        'b_mod': b_mod,
        'w_in': nrm(ks[12], (DEPTH, D, IN_COLS), D ** -0.5),
        'b_in': b_in,
        'mla_q_norm': 1.0 + nrm(ks[14], (DEPTH, Q_RANK), 0.02),
        'w_uq': nrm(ks[15], (DEPTH, Q_RANK, MLA_H * (MLA_DN + MLA_DR)), Q_RANK ** -0.5),
        'mla_kv_norm': 1.0 + nrm(ks[16], (DEPTH, KV_RANK), 0.02),
        'w_ukv': nrm(ks[17], (DEPTH, KV_RANK, MLA_H * (MLA_DN + MLA_DV)), KV_RANK ** -0.5),
        'mlstm_norm': 1.0 + nrm(ks[18], (DEPTH, ML_W), 0.02),
        'hgrn_lb_logits': 1.0 + nrm(ks[19], (DEPTH, HG_KW), 0.1),
        'hgrn_norm': 1.0 + nrm(ks[20], (DEPTH, HG_W), 0.02),
        'w_out': nrm(ks[21], (DEPTH, MIX_W, D), BETA * MIX_W ** -0.5),
        'ln1_g': 1.0 + nrm(ks[22], (DEPTH, D), 0.02),
        'ln1_b': nrm(ks[23], (DEPTH, D), 0.02),
        'w_ffn_in': nrm(ks[24], (DEPTH, D, 2 * FF), D ** -0.5),
        'w_ffn_out': nrm(ks[25], (DEPTH, FF, D), BETA * FF ** -0.5),
        'ln2_g': 1.0 + nrm(ks[26], (DEPTH, D), 0.02),
        'ln2_b': nrm(ks[27], (DEPTH, D), 0.02),
    }


def reference(x_prompt, x_sample, cache_mla_ckv, cache_mla_kpe, state_mlstm_C, state_mlstm_n, state_mlstm_m,
              state_hgrn_S, c, c_ctx, w_mod, b_mod, w_in, b_in, mla_q_norm, w_uq, mla_kv_norm, w_ukv,
              mlstm_norm, hgrn_lb_logits, hgrn_norm, w_out, ln1_g, ln1_b, w_ffn_in, w_ffn_out, ln2_g, ln2_b):
    lb_p = jax.nn.softmax(hgrn_lb_logits.astype(jnp.float32), axis=0)
    lbs = jnp.cumsum(lb_p, axis=0) - lb_p[0]
    rope = axial_rope(x_sample.shape[1])
    yp = x_prompt
    ys = x_sample
    ckv_l, kpe_l, mC_l, mn_l, mm_l, hS_l = [], [], [], [], [], []
    for l in range(DEPTH):
        lp = dict(w_mod=w_mod[l], b_mod=b_mod[l], w_in=w_in[l], b_in=b_in[l],
                  mla_q_norm=mla_q_norm[l], w_uq=w_uq[l], mla_kv_norm=mla_kv_norm[l], w_ukv=w_ukv[l],
                  mlstm_norm=mlstm_norm[l], hgrn_norm=hgrn_norm[l], w_out=w_out[l],
                  ln1_g=ln1_g[l], ln1_b=ln1_b[l], w_ffn_in=w_ffn_in[l], w_ffn_out=w_ffn_out[l],
                  ln2_g=ln2_g[l], ln2_b=ln2_b[l])
        yp, (mla_st, ml_st, hg_st) = trunk_layer(yp, c_ctx[None, :], lp, lbs[l], None, None)
        ckv_l.append(mla_st[0])
        kpe_l.append(mla_st[1])
        mC_l.append(ml_st[0])
        mn_l.append(ml_st[1])
        mm_l.append(ml_st[2])
        hS_l.append(hg_st)
        cache_l = ((cache_mla_ckv[:, l], cache_mla_kpe[:, l]),
                   ((state_mlstm_C[:, l, 0], state_mlstm_n[:, l, 0], state_mlstm_m[:, l, 0]),
                    (state_mlstm_C[:, l, 1], state_mlstm_n[:, l, 1], state_mlstm_m[:, l, 1])),
                   (state_hgrn_S[:, l, 0], state_hgrn_S[:, l, 1]))
        ys, _ = trunk_layer(ys, c, lp, lbs[l], cache_l, rope)
    new_mla_ckv = jnp.stack(ckv_l, axis=1)
    new_mla_kpe = jnp.stack(kpe_l, axis=1)
    new_mlstm_C = jnp.stack(mC_l, axis=1)
    new_mlstm_n = jnp.stack(mn_l, axis=1)
    new_mlstm_m = jnp.stack(mm_l, axis=1)
    new_hgrn_S = jnp.stack(hS_l, axis=1)
    return (yp, ys, new_mla_ckv, new_mla_kpe, new_mlstm_C, new_mlstm_n, new_mlstm_m, new_hgrn_S)
```

```python
import functools

import numpy as np
import jax
import jax.numpy as jnp
from jax import lax
from jax.experimental import pallas as pl
from jax.experimental.pallas import tpu as pltpu

F32 = jnp.float32
BF16 = jnp.bfloat16

D_MODEL = 1024
DEPTH = 4
GRID_W = 64
MLA_H = 8
MLA_DN = 64
MLA_DR = 32
MLA_DV = 64
Q_RANK = 256
KV_RANK = 128
N_HEADS = 4
HEAD_D = 64
REC_W = N_HEADS * HEAD_D
FF = 2816
CHUNK = 64
ROPE_BASE = 10000.0
ALPHA = (2 * DEPTH) ** 0.25
EPS = 1e-6
TINY = 1e-30
NEG_BIG = -1e30
IN_SIZES = (256, 128, 32, 256, 256, 256, 256, 4, 4, 4, 4, 256, 256, 256, 256, 256)
LANE = 128
Q_SLOT = 2 * LANE
PROJ_W = 24 * LANE
VMEM_LIMIT = 56 * 1024 * 1024

BLK_CQ, BLK_KV, BLK_MQ, BLK_MK, BLK_MV, BLK_MO, BLK_GQ, BLK_GFF, BLK_GFB, BLK_GI, BLK_GG = range(11)
BLK_GATE_I, BLK_GATE_F = 22, 23


def _params(*sem):
    return pltpu.CompilerParams(dimension_semantics=sem, vmem_limit_bytes=VMEM_LIMIT)


def _split3(x):
    x1 = x.astype(BF16)
    r1 = x - x1.astype(F32)
    x2 = r1.astype(BF16)
    x3 = (r1 - x2.astype(F32)).astype(BF16)
    return x1, x2, x3


def _dot(a, b):
    return jnp.dot(a, b, preferred_element_type=F32)


def _dot_t(a, b):
    return lax.dot_general(a, b, (((1,), (1,)), ((), ())), preferred_element_type=F32)


def _sel_rows(sel3, x):
    return _dot(sel3, jnp.concatenate(_split3(x), axis=0))


def _sel_cols(x, sel3):
    return _dot(jnp.concatenate(_split3(x), axis=1), sel3)


def _seg_sum(x, bd2):
    x1 = x.astype(BF16)
    x2 = (x - x1.astype(F32)).astype(BF16)
    return _dot(jnp.concatenate([x1, x2], axis=1), bd2)


def _block_diag(x_bf16, bd):
    return jnp.concatenate([x_bf16] * N_HEADS, axis=0) * bd


def _log_sigmoid(x):
    return jnp.minimum(x, 0.0) - jnp.log1p(jnp.exp(-jnp.abs(x)))


def _silu(x):
    return x * jax.nn.sigmoid(x)


def _layernorm(y, g, b):
    mu = jnp.mean(y, axis=-1, keepdims=True)
    yc = y - mu
    var = jnp.mean(yc * yc, axis=-1, keepdims=True)
    return yc * lax.rsqrt(var + EPS) * g + b


def _recurrent_constants():
    L = CHUNK
    t = np.arange(L)
    tri = (t[None, :] <= t[:, None]).astype(np.float32)
    cum = np.stack([tri, tri.T])
    cum3 = np.tile(cum, (1, 1, 3))

    lane = np.arange(REC_W)
    exp = np.zeros((2, LANE, REC_W), np.float32)
    for d in range(2):
        exp[d, d * N_HEADS + lane // HEAD_D, lane] = 1.0
    exp3 = np.tile(exp, (1, 3, 1))

    bd = (lane[:, None] // HEAD_D == lane[None, :] // HEAD_D).astype(np.float32)
    bd2 = np.tile(bd, (2, 1))
    itile = (t[:, None] == (lane % HEAD_D)[None, :]).astype(np.float32)
    s_of_lane = lane % HEAD_D
    caus = np.stack([(s_of_lane[None, :] <= t[:, None]), (s_of_lane[None, :] >= t[:, None])]).astype(np.float32)

    def stage(G, g):
        ka = np.zeros((L, L), np.float32)
        qa = np.zeros((3, L, L), np.float32)
        msk = np.zeros((3, L, L), np.float32)
        for s in range(L):
            e = (s // g) * g + g - 1
            ka[s, s + 1:e + 1] = 1.0
        for tt in range(L):
            p = (tt % G) // g
            for j in range(3):
                if p > j:
                    E = (tt // G) * G + (j + 1) * g - 1
                    qa[j, tt, E + 1:tt + 1] = 1.0
                    for s in range(L):
                        if s // G == tt // G and (s % G) // g == j:
                            msk[j, tt, s] = 1.0
        return ka, qa, msk

    mats, msks = [], []
    for d in range(2):
        perm = t if d == 0 else t[::-1]

        def mir(m):
            return m[np.ix_(perm, perm)]

        ka16, qa16, m16 = stage(64, 16)
        ka4, qa4, m4 = stage(16, 4)
        _, qa1, m1 = stage(4, 1)
        blocks = [tri, ka16, qa16[0], qa16[1], qa16[2], ka4, qa4[0], qa4[1], qa4[2], qa1[0], qa1[1], qa1[2]]
        mats.append(np.concatenate([mir(b) for b in blocks], axis=0))
        mk = [m16[0], m16[1], m16[2], m4[0], m4[1], m4[2], m1[0], m1[1], m1[2], np.eye(L, dtype=np.float32)]
        msks.append(np.stack([np.tile(mir(m), (1, N_HEADS)) for m in mk]))
    mat3 = np.tile(np.stack(mats), (1, 1, 3))
    msk = np.stack(msks)
    return dict(
        cum3=jnp.asarray(cum3, BF16), exp3=jnp.asarray(exp3, BF16), bd=jnp.asarray(bd, BF16),
        bd2=jnp.asarray(bd2, BF16), itile=jnp.asarray(itile, F32), caus=jnp.asarray(caus, F32),
        mat3=jnp.asarray(mat3, BF16), msk=jnp.asarray(msk, F32))


def _mod_kernel(c_ref, w_ref, b_ref, o_ref):
    a = _silu(c_ref[...]).astype(BF16)
    o_ref[...] = _dot(a, w_ref[...].astype(BF16)) + b_ref[...]


def _modulation(cvec8, w_mod, b_mod):
    D = D_MODEL
    return pl.pallas_call(
        _mod_kernel,
        grid=(DEPTH, 6),
        in_specs=[
            pl.BlockSpec((8, D), lambda l, j: (0, 0)),
            pl.BlockSpec((None, D, D), lambda l, j: (l, 0, j)),
            pl.BlockSpec((None, None, 1, D), lambda l, j: (l, j, 0, 0)),
        ],
        out_specs=pl.BlockSpec((None, None, 8, D), lambda l, j: (l, j, 0, 0)),
        out_shape=jax.ShapeDtypeStruct((DEPTH, 6, 8, D), F32),
        compiler_params=_params("parallel", "parallel"),
        name="modulation",
    )(cvec8, w_mod, b_mod.reshape(DEPTH, 6, 1, D))


def _lb_kernel(x_ref, o_ref):
    x = x_ref[...]
    e = jnp.exp(x - jnp.max(x, axis=0, keepdims=True))
    p = e / jnp.sum(e, axis=0, keepdims=True)
    acc = jnp.zeros_like(p[0:1])
    rows = []
    for l in range(DEPTH):
        acc = acc + p[l:l + 1]
        rows.append(acc - p[0:1])
    o_ref[...] = jnp.concatenate(rows, axis=0)


def _hgrn_lower_bounds(logits):
    return pl.pallas_call(
        _lb_kernel, out_shape=jax.ShapeDtypeStruct(logits.shape, F32), name="hgrn_lower_bounds",
    )(logits.astype(F32))


def _absorb_kernel(a_ref, b_ref, o_ref):
    a1, a2, a3 = _split3(a_ref[...])
    b1, b2, b3 = _split3(b_ref[...])
    o_ref[...] = (_dot_t(a1, b1) + _dot_t(a1, b2) + _dot_t(a2, b1)
                  + _dot_t(a1, b3) + _dot_t(a2, b2) + _dot_t(a3, b1))


def _absorbed_query_weights(wq_nope, wk_nope):
    return pl.pallas_call(
        _absorb_kernel,
        grid=(DEPTH, MLA_H),
        in_specs=[
            pl.BlockSpec((None, None, Q_RANK, MLA_DN), lambda l, h: (l, h, 0, 0)),
            pl.BlockSpec((None, None, KV_RANK, MLA_DN), lambda l, h: (l, h, 0, 0)),
        ],
        out_specs=pl.BlockSpec((None, None, Q_RANK, KV_RANK), lambda l, h: (l, h, 0, 0)),
        out_shape=jax.ShapeDtypeStruct((DEPTH, MLA_H, Q_RANK, KV_RANK), F32),
        compiler_params=_params("parallel", "parallel"),
        name="absorb_query_weights",
    )(wq_nope, wk_nope)


def _inproj_kernel(x_ref, mod_ref, w_ref, b_ref, o_ref):
    sh, sc = mod_ref[0, 0:1, :], mod_ref[0, 1:2, :]
    h = (x_ref[...] * (1.0 + sc) + sh).astype(BF16)
    o_ref[...] = _dot(h, w_ref[...]) + b_ref[...]


def _inproj(x, mod, w, b, *, tm, tiles_per_mod):
    n = x.shape[0]
    return pl.pallas_call(
        _inproj_kernel,
        grid=(n // tm,),
        in_specs=[
            pl.BlockSpec((tm, D_MODEL), lambda i: (i, 0)),
            pl.BlockSpec((1, 6, D_MODEL), lambda i: (i // tiles_per_mod, 0, 0)),
            pl.BlockSpec((D_MODEL, PROJ_W), lambda i: (0, 0)),
            pl.BlockSpec((1, PROJ_W), lambda i: (0, 0)),
        ],
        out_specs=pl.BlockSpec((tm, PROJ_W), lambda i: (i, 0)),
        out_shape=jax.ShapeDtypeStruct((n, PROJ_W), F32),
        compiler_params=_params("parallel"),
        name="in_projection",
    )(x, mod, w, b)


def _attn_prep_kernel(cq_ref, kv_ref, qtab_ref, ktab_ref, wq_ref, qg_ref, kg_ref, q_ref, k_ref, ckv_ref):
    cq = cq_ref[...]
    cqn = cq * lax.rsqrt(jnp.mean(cq * cq, axis=-1, keepdims=True) + EPS) * qg_ref[...]
    qf = _dot(cqn.astype(BF16), wq_ref[...])
    qtab = qtab_ref[...]
    for h in range(MLA_H):
        q_ref[:, h * Q_SLOT:(h + 1) * Q_SLOT] = (qf[:, h * Q_SLOT:(h + 1) * Q_SLOT] * qtab).astype(BF16)
    ckv = kv_ref[:, :KV_RANK]
    ckvn = ckv * lax.rsqrt(jnp.mean(ckv * ckv, axis=-1, keepdims=True) + EPS) * kg_ref[...]
    ckv_ref[...] = ckvn
    k_ref[:, :KV_RANK] = ckvn.astype(BF16)
    t = kv_ref[:, KV_RANK:] * ktab_ref[...]
    kp = t + pltpu.roll(t, MLA_DR, 1) + pltpu.roll(t, LANE - MLA_DR, 1)
    lane = lax.broadcasted_iota(jnp.int32, kp.shape, 1)
    k_ref[:, KV_RANK:] = jnp.where(lane < 2 * MLA_DR, kp, 0.0).astype(BF16)


def _attn_prep(proj, qtab, ktab, wq, qg, kg, *, tm, tab_tiles):
    n = proj.shape[0]
    return pl.pallas_call(
        _attn_prep_kernel,
        grid=(n // tm,),
        in_specs=[
            pl.BlockSpec((tm, Q_RANK), lambda i: (i, BLK_CQ)),
            pl.BlockSpec((tm, 2 * LANE), lambda i: (i, BLK_KV)),
            pl.BlockSpec((tm, Q_SLOT), lambda i: (i % tab_tiles, 0)),
            pl.BlockSpec((tm, LANE), lambda i: (i % tab_tiles, 0)),
            pl.BlockSpec((Q_RANK, MLA_H * Q_SLOT), lambda i: (0, 0)),
            pl.BlockSpec((1, Q_RANK), lambda i: (0, 0)),
            pl.BlockSpec((1, KV_RANK), lambda i: (0, 0)),
        ],
        out_specs=[
            pl.BlockSpec((tm, MLA_H * Q_SLOT), lambda i: (i, 0)),
            pl.BlockSpec((tm, Q_SLOT), lambda i: (i, 0)),
            pl.BlockSpec((tm, KV_RANK), lambda i: (i, 0)),
        ],
        out_shape=[
            jax.ShapeDtypeStruct((n, MLA_H * Q_SLOT), BF16),
            jax.ShapeDtypeStruct((n, Q_SLOT), BF16),
            jax.ShapeDtypeStruct((n, KV_RANK), F32),
        ],
        compiler_params=_params("parallel"),
        name="attention_prep",
    )(proj, proj, qtab, ktab, wq, qg, kg)


def _attn_kernel(*refs, has_cache):
    if has_cache:
        q_ref, k_ref, kc_ref, o_ref = refs
    else:
        q_ref, k_ref, o_ref = refs
    q = q_ref[...]
    k = k_ref[...]
    s = _dot_t(q, k)
    m = jnp.max(s, axis=-1, keepdims=True)
    if has_cache:
        kc = kc_ref[...]
        sc = _dot_t(q, kc)
        m = jnp.maximum(m, jnp.max(sc, axis=-1, keepdims=True))
    p = jnp.exp(s - m)
    l = jnp.sum(p, axis=-1, keepdims=True)
    o = _dot(p.astype(BF16), k[:, :KV_RANK])
    if has_cache:
        pc = jnp.exp(sc - m)
        l = l + jnp.sum(pc, axis=-1, keepdims=True)
        o = o + _dot(pc.astype(BF16), kc[:, :KV_RANK])
    o_ref[...] = (o / l).astype(BF16)


def _attention(q, k, kcache, *, n_seq, seq_len, tq):
    n = q.shape[0]
    qt = seq_len // tq
    in_specs = [
        pl.BlockSpec((tq, Q_SLOT), lambda b, i, h: (b * qt + i, h)),
        pl.BlockSpec((seq_len, Q_SLOT), lambda b, i, h: (b, 0)),
    ]
    args = [q, k]
    if kcache is not None:
        in_specs.append(pl.BlockSpec((None, kcache.shape[1], Q_SLOT), lambda b, i, h: (b, 0, 0)))
        args.append(kcache)
    return pl.pallas_call(
        functools.partial(_attn_kernel, has_cache=kcache is not None),
        grid=(n_seq, qt, MLA_H),
        in_specs=in_specs,
        out_specs=pl.BlockSpec((tq, KV_RANK), lambda b, i, h: (b * qt + i, h)),
        out_shape=jax.ShapeDtypeStruct((n, MLA_H * KV_RANK), BF16),
        compiler_params=_params("parallel", "parallel", "parallel"),
        name="attention",
    )(*args)


def _outproj_kernel(x_ref, ol_ref, mg_ref, mod_ref, wuv_ref, wo_ref, g_ref, b_ref, o_ref):
    a = _dot(ol_ref[...], wuv_ref[...]).astype(BF16)
    mix = _dot(a, wo_ref[:MLA_H * MLA_DV, :]) + _dot(mg_ref[...], wo_ref[MLA_H * MLA_DV:, :])
    g1 = mod_ref[0, 2:3, :]
    y = ALPHA * x_ref[...] + g1 * mix
    o_ref[...] = _layernorm(y, g_ref[...], b_ref[...])


def _outproj(x, o_lat, mg, mod, wuv, wo, g, b, *, tm, tiles_per_mod):
    n = x.shape[0]
    D = D_MODEL
    return pl.pallas_call(
        _outproj_kernel,
        grid=(n // tm,),
        in_specs=[
            pl.BlockSpec((tm, D), lambda i: (i, 0)),
            pl.BlockSpec((tm, MLA_H * KV_RANK), lambda i: (i, 0)),
            pl.BlockSpec((tm, 2 * REC_W), lambda i: (i, 0)),
            pl.BlockSpec((1, 6, D), lambda i: (i // tiles_per_mod, 0, 0)),
            pl.BlockSpec((MLA_H * KV_RANK, MLA_H * MLA_DV), lambda i: (0, 0)),
            pl.BlockSpec((D, D), lambda i: (0, 0)),
            pl.BlockSpec((1, D), lambda i: (0, 0)),
            pl.BlockSpec((1, D), lambda i: (0, 0)),
        ],
        out_specs=pl.BlockSpec((tm, D), lambda i: (i, 0)),
        out_shape=jax.ShapeDtypeStruct((n, D), F32),
        compiler_params=_params("parallel"),
        name="out_projection",
    )(x, o_lat, mg, mod, wuv, wo, g, b)


FF_STEPS = 2
FF_BLK = FF // FF_STEPS


def _ffn_kernel(x_ref, mod_ref, wg_ref, wu_ref, wo_ref, g_ref, b_ref, o_ref, h_scr, acc_scr):
    j = pl.program_id(1)

    @pl.when(j == 0)
    def _():
        sh, sc = mod_ref[0, 3:4, :], mod_ref[0, 4:5, :]
        h_scr[...] = (x_ref[...] * (1.0 + sc) + sh).astype(BF16)
        acc_scr[...] = jnp.zeros_like(acc_scr)

    h = h_scr[...]
    act = (_silu(_dot(h, wg_ref[...])) * _dot(h, wu_ref[...])).astype(BF16)
    acc_scr[...] += _dot(act, wo_ref[...])

    @pl.when(j == FF_STEPS - 1)
    def _():
        g2 = mod_ref[0, 5:6, :]
        y = ALPHA * x_ref[...] + g2 * acc_scr[...]
        o_ref[...] = _layernorm(y, g_ref[...], b_ref[...])


def _ffn(x, mod, w_in, w_out, g, b, *, tm, tiles_per_mod):
    n = x.shape[0]
    D = D_MODEL
    return pl.pallas_call(
        _ffn_kernel,
        grid=(n // tm, FF_STEPS),
        in_specs=[
            pl.BlockSpec((tm, D), lambda i, j: (i, 0)),
            pl.BlockSpec((1, 6, D), lambda i, j: (i // tiles_per_mod, 0, 0)),
            pl.BlockSpec((D, FF_BLK), lambda i, j: (0, j)),
            pl.BlockSpec((D, FF_BLK), lambda i, j: (0, FF_STEPS + j)),
            pl.BlockSpec((FF_BLK, D), lambda i, j: (j, 0)),
            pl.BlockSpec((1, D), lambda i, j: (0, 0)),
            pl.BlockSpec((1, D), lambda i, j: (0, 0)),
        ],
        out_specs=pl.BlockSpec((tm, D), lambda i, j: (i, 0)),
        out_shape=jax.ShapeDtypeStruct((n, D), F32),
        scratch_shapes=[pltpu.VMEM((tm, D), BF16), pltpu.VMEM((tm, D), F32)],
        compiler_params=_params("parallel", "arbitrary"),
        name="ffn",
    )(x, mod, w_in, w_in, w_out, g, b)


def _gate_terms(gti_f, gtf_f, gti_b, gtf_b, cum3_ref):
    lane = lax.broadcasted_iota(jnp.int32, (CHUNK, LANE), 1)
    is_f = lane < N_HEADS
    ig = jnp.where(is_f, gti_f, gti_b)
    lf = _log_sigmoid(jnp.where(is_f, gtf_f, gtf_b))
    b = jnp.where(is_f, _sel_rows(cum3_ref[0], lf), _sel_rows(cum3_ref[1], lf))
    return is_f, ig, lf, b


def _hgrn_gates(fr, lb):
    f = lb + (1.0 - lb) * jax.nn.sigmoid(fr)
    kk = (1.0 - lb) * jax.nn.sigmoid(-fr)
    return kk, jnp.log(jnp.maximum(f, TINY))


def _scan_kernel(*refs, nc, zero_init, emit_final):
    it = iter(refs)
    mk = (next(it), next(it))
    mv = (next(it), next(it))
    gf = (next(it), next(it))
    gi = (next(it), next(it))
    gti = (next(it), next(it))
    gtf = (next(it), next(it))
    lb_ref, cum3_ref, exp3_ref, bd_ref = next(it), next(it), next(it), next(it)
    if not zero_init:
        c0_ref, n0_ref, m0_ref, s0_ref = next(it), next(it), next(it), next(it)
    cs = (next(it), next(it))
    ns = (next(it), next(it))
    ms = (next(it), next(it))
    ss = (next(it), next(it))
    if emit_final:
        cfin_ref, nfin_ref, mfin_ref, sfin_ref = next(it), next(it), next(it), next(it)
    c_scr, n_scr, m_scr, s_scr = next(it), next(it), next(it), next(it)

    pos = pl.program_id(0) % nc

    @pl.when(pos == 0)
    def _():
        if zero_init:
            c_scr[...] = jnp.zeros_like(c_scr)
            n_scr[...] = jnp.zeros_like(n_scr)
            m_scr[...] = jnp.zeros_like(m_scr)
            s_scr[...] = jnp.zeros_like(s_scr)
        else:
            c_scr[...] = c0_ref[...]
            n_scr[...] = n0_ref[...]
            m_scr[...] = m0_ref[...]
            s_scr[...] = s0_ref[...]

    bd = bd_ref[...]
    _, ig, lf, b = _gate_terms(gti[0][...], gtf[0][...], gti[1][...], gtf[1][...], cum3_ref)
    r = ig - b
    rmax = jnp.max(r, axis=0, keepdims=True)
    g = jnp.sum(lf, axis=0, keepdims=True)
    m = m_scr[...]
    mm = jnp.maximum(m, rmax)
    sc = jnp.exp(m - mm)
    w = jnp.exp(r - mm)
    ms[0][...] = m
    ms[1][...] = m
    m_scr[...] = g + mm
    wsc = jnp.concatenate([w, jnp.broadcast_to(sc, (16, LANE))], axis=0)
    lb = lb_ref[...]
    for d in range(2):
        c_old = c_scr[d]
        n_old = n_scr[d]
        s_old = s_scr[d]
        cs[d][...] = c_old.astype(BF16) * bd
        ns[d][...] = n_old
        ss[d][...] = s_old.astype(BF16) * bd
        wx = _sel_cols(wsc, exp3_ref[d])
        scx = wx[CHUNK:CHUNK + 1]
        kw = mk[d][...] * (HEAD_D ** -0.5) * wx[:CHUNK]
        c_new = c_old * scx + _dot(kw.T.astype(BF16), mv[d][...].astype(BF16))
        n_new = n_old * scx + jnp.sum(kw, axis=0, keepdims=True)
        c_scr[d] = c_new
        n_scr[d] = n_new
        kk, lg = _hgrn_gates(gf[d][...], lb)
        bc = _sel_rows(cum3_ref[d], lg)
        gl = jnp.sum(lg, axis=0, keepdims=True)
        kd = kk * jnp.exp(gl - bc)
        s_new = s_old * jnp.exp(gl) + _dot(gi[d][...].T.astype(BF16), kd.astype(BF16))
        s_scr[d] = s_new
        if emit_final:
            @pl.when(pos == nc - 1)
            def _():
                cfin_ref[d] = c_new
                nfin_ref[d] = n_new
                sfin_ref[d] = s_new
    if emit_final:
        @pl.when(pos == nc - 1)
        def _():
            mfin_ref[...] = g + mm


def _state_scan(proj, lb, consts, init, *, n_seq, nc, emit_final):
    n_chunks = n_seq * nc
    zero_init = init is None

    def fwd(c):
        return c

    def bwd(c):
        return (c // nc) * nc + (nc - 1 - c % nc)

    def blk(col, w, idx):
        return pl.BlockSpec((CHUNK, w), lambda c: (idx(c), col))

    in_specs, args = [], []
    for col, w in ((BLK_MK, REC_W), (BLK_MV, REC_W)):
        for idx in (fwd, bwd):
            in_specs.append(blk(col, w, idx))
            args.append(proj)
    in_specs += [blk(BLK_GFF, REC_W, fwd), blk(BLK_GFB, REC_W, bwd)]
    args += [proj, proj]
    for col, w in ((BLK_GI, REC_W), (BLK_GATE_I, LANE), (BLK_GATE_F, LANE)):
        for idx in (fwd, bwd):
            in_specs.append(blk(col, w, idx))
            args.append(proj)
    in_specs += [
        pl.BlockSpec((1, REC_W), lambda c: (0, 0)),
        pl.BlockSpec((2, CHUNK, 3 * CHUNK), lambda c: (0, 0, 0)),
        pl.BlockSpec((2, 3 * LANE, REC_W), lambda c: (0, 0, 0)),
        pl.BlockSpec((REC_W, REC_W), lambda c: (0, 0)),
    ]
    args += [lb, consts["cum3"], consts["exp3"], consts["bd"]]
    if not zero_init:
        in_specs += [
            pl.BlockSpec((None, 2, REC_W, REC_W), lambda c: (c // nc, 0, 0, 0)),
            pl.BlockSpec((None, 2, 1, REC_W), lambda c: (c // nc, 0, 0, 0)),
            pl.BlockSpec((None, 1, LANE), lambda c: (c // nc, 0, 0)),
            pl.BlockSpec((None, 2, REC_W, REC_W), lambda c: (c // nc, 0, 0, 0)),
        ]
        args += list(init)

    out_specs, out_shape = [], []

    def add_out(shape, dtype):
        for idx in (fwd, bwd):
            out_specs.append(pl.BlockSpec((None,) + shape, lambda c, idx=idx: (idx(c),) + (0,) * len(shape)))
            out_shape.append(jax.ShapeDtypeStruct((n_chunks,) + shape, dtype))

    add_out((REC_W, REC_W), BF16)
    add_out((1, REC_W), F32)
    add_out((1, LANE), F32)
    add_out((REC_W, REC_W), BF16)
    if emit_final:
        for shape in ((2, REC_W, REC_W), (2, 1, REC_W), (1, LANE), (2, REC_W, REC_W)):
            out_specs.append(pl.BlockSpec((None,) + shape, lambda c, k=len(shape): (c // nc,) + (0,) * k))
            out_shape.append(jax.ShapeDtypeStruct((n_seq,) + shape, F32))

    return pl.pallas_call(
        functools.partial(_scan_kernel, nc=nc, zero_init=zero_init, emit_final=emit_final),
        grid=(n_chunks,),
        in_specs=in_specs,
        out_specs=out_specs,
        out_shape=out_shape,
        scratch_shapes=[
            pltpu.VMEM((2, REC_W, REC_W), F32),
            pltpu.VMEM((2, 1, REC_W), F32),
            pltpu.VMEM((1, LANE), F32),
            pltpu.VMEM((2, REC_W, REC_W), F32),
        ],
        compiler_params=_params("arbitrary"),
        name="state_scan",
    )(*args)


def _mixer_kernel(mq_ref, mk_ref, mv_ref, mo_ref, gq_ref, gff_ref, gfb_ref, gi_ref, gg_ref, gti_ref, gtf_ref,
                  csf_ref, csb_ref, nsf_ref, nsb_ref, msf_ref, msb_ref, ssf_ref, ssb_ref,
                  lb_ref, mnorm_ref, hnorm_ref,
                  cum3_ref, exp3_ref, bd_ref, bd2_ref, itile_ref, caus_ref, mat3_ref, msk_ref,
                  o_ref):
    bd = bd_ref[...]
    bd2 = bd2_ref[...]
    inv_d = 1.0 / HEAD_D

    gti = gti_ref[...]
    gtf = gtf_ref[...]
    is_f, ig, lf, b = _gate_terms(gti, gtf, gti, gtf, cum3_ref)
    r = ig - b
    row = lax.broadcasted_iota(jnp.int32, (CHUNK, LANE), 0)
    cm_f = r
    cm_b = r
    sh = 1
    while sh < CHUNK:
        cm_f = jnp.maximum(cm_f, jnp.where(row >= sh, pltpu.roll(cm_f, sh, 0), NEG_BIG))
        cm_b = jnp.maximum(cm_b, jnp.where(row < CHUNK - sh, pltpu.roll(cm_b, CHUNK - sh, 0), NEG_BIG))
        sh *= 2
    m_row = jnp.where(is_f[0:1], msf_ref[...], msb_ref[...])
    big_m = jnp.maximum(m_row, jnp.where(is_f, cm_f, cm_b))
    ws = jnp.exp(m_row - big_m)
    stacked = jnp.concatenate([big_m, ws, b + big_m, r], axis=0)

    q = mq_ref[...]
    qb = q.astype(BF16)
    k_bd = _block_diag((mk_ref[...] * (HEAD_D ** -0.5)).astype(BF16), bd)
    v_bd = _block_diag(mv_ref[...].astype(BF16), bd)
    s = _dot_t(qb, k_bd)
    ones3 = jnp.ones((CHUNK, 3 * CHUNK), BF16)
    itile = itile_ref[...]
    p = []
    ex = []
    for d in range(2):
        x = _sel_cols(stacked, exp3_ref[d])
        ex.append(x)
        r_bcast = _sel_rows(ones3, x[3 * CHUNK:] * itile)
        arg = jnp.where(caus_ref[d] > 0.0, r_bcast - x[:CHUNK], NEG_BIG)
        p.append(s * jnp.exp(arg))
    pv = _dot(jnp.concatenate(p, axis=0).astype(BF16), v_bd)
    hsum = None
    for d, (c_ref, n_ref) in enumerate(((csf_ref, nsf_ref), (csb_ref, nsb_ref))):
        wsx = ex[d][CHUNK:2 * CHUNK]
        num = wsx * _dot(qb, c_ref[...]) + pv[d * CHUNK:(d + 1) * CHUNK]
        den = wsx * _seg_sum(q * n_ref[...], bd2) + _seg_sum(p[d], bd2)
        hd = num / jnp.maximum(jnp.abs(den), jnp.exp(-ex[d][2 * CHUNK:3 * CHUNK]))
        hsum = hd if hsum is None else hsum + hd
    mu = _seg_sum(hsum, bd2) * inv_d
    hc = hsum - mu
    var = _seg_sum(hc * hc, bd2) * inv_d
    m_out = hc * lax.rsqrt(var + EPS) * mnorm_ref[...] * jax.nn.sigmoid(mo_ref[...])
    o_ref[:, :REC_W] = m_out.astype(BF16)

    qh = _silu(gq_ref[...])
    lb = lb_ref[...]
    gv_bd = _block_diag(gi_ref[...].astype(BF16), bd)
    a_tot = None
    inter = None
    for d, (f_ref, s_ref) in enumerate(((gff_ref, ssf_ref), (gfb_ref, ssb_ref))):
        kk, lg = _hgrn_gates(f_ref[...], lb)
        e = jnp.exp(_sel_rows(mat3_ref[d], lg))

        def eb(i):
            return e[i * CHUNK:(i + 1) * CHUNK]

        t_in = _dot_t((qh * eb(0)).astype(BF16), s_ref[...])
        inter = t_in if inter is None else inter + t_in
        stages = (
            (kk * eb(1), [qh * eb(2), qh * eb(3), qh * eb(4)]),
            (kk * eb(5), [qh * eb(6), qh * eb(7), qh * eb(8)]),
            (kk, [qh * eb(9), qh * eb(10), qh * eb(11), qh]),
        )
        mi = 0
        for k_st, q_list in stages:
            out = _dot_t(jnp.concatenate(q_list, axis=0).astype(BF16), _block_diag(k_st.astype(BF16), bd))
            for j in range(len(q_list)):
                term = msk_ref[d, mi] * out[j * CHUNK:(j + 1) * CHUNK]
                a_tot = term if a_tot is None else a_tot + term
                mi += 1
    o = _dot(a_tot.astype(BF16), gv_bd) + inter
    ms = _seg_sum(o * o, bd2) * inv_d
    g_out = o * lax.rsqrt(ms + EPS) * hnorm_ref[...] * _silu(gg_ref[...])
    o_ref[:, REC_W:] = g_out.astype(BF16)


def _mixer_outputs(proj, states, lb, mnorm, hnorm, consts):
    n = proj.shape[0]
    n_chunks = n // CHUNK

    def blk(col, w):
        return pl.BlockSpec((CHUNK, w), lambda c: (c, col))

    in_specs = [blk(c, REC_W) for c in (BLK_MQ, BLK_MK, BLK_MV, BLK_MO, BLK_GQ, BLK_GFF, BLK_GFB, BLK_GI, BLK_GG)]
    in_specs += [blk(BLK_GATE_I, LANE), blk(BLK_GATE_F, LANE)]
    args = [proj] * 11
    csf, csb, nsf, nsb, msf, msb, ssf, ssb = states
    for arr in (csf, csb, nsf, nsb, msf, msb, ssf, ssb):
        shape = arr.shape[1:]
        in_specs.append(pl.BlockSpec((None,) + shape, lambda c, k=len(shape): (c,) + (0,) * k))
        args.append(arr)

    def whole(arr):
        return pl.BlockSpec(arr.shape, lambda c, k=arr.ndim: (0,) * k)

    for arr in (lb, mnorm, hnorm, consts["cum3"], consts["exp3"], consts["bd"], consts["bd2"], consts["itile"],
                consts["caus"], consts["mat3"], consts["msk"]):
        in_specs.append(whole(arr))
        args.append(arr)
    return pl.pallas_call(
        _mixer_kernel,
        grid=(n_chunks,),
        in_specs=in_specs,
        out_specs=pl.BlockSpec((CHUNK, 2 * REC_W), lambda c: (c, 0)),
        out_shape=jax.ShapeDtypeStruct((n, 2 * REC_W), BF16),
        compiler_params=_params("parallel"),
        name="mixer_outputs",
    )(*args)


def _rot_perm():
    q = MLA_DR // 4
    idx = np.concatenate([np.arange(q, 2 * q), np.arange(0, q), np.arange(3 * q, 4 * q), np.arange(2 * q, 3 * q)])
    sign = np.concatenate([-np.ones(q), np.ones(q), -np.ones(q), np.ones(q)]).astype(np.float32)
    return idx, sign


def _pad_cols(a, width):
    return jnp.pad(a, [(0, 0)] * (a.ndim - 1) + [(0, width - a.shape[-1])])


def _layout_in_proj(w_in, b_in):
    off = np.concatenate([[0], np.cumsum(IN_SIZES)])
    idx, sign = _rot_perm()

    def lay(a):
        def cols(i, j=None):
            return a[..., int(off[i]):int(off[(i if j is None else j) + 1])]

        kpe = cols(2)
        return jnp.concatenate([
            cols(0), cols(1),
            _pad_cols(jnp.concatenate([kpe, kpe[..., idx] * sign], axis=-1), LANE),
            cols(3, 6), cols(11, 15),
            _pad_cols(jnp.concatenate([cols(7), cols(8)], axis=-1), LANE),
            _pad_cols(jnp.concatenate([cols(9), cols(10)], axis=-1), LANE),
        ], axis=-1)

    return lay(w_in).astype(BF16), lay(b_in)[:, None, :]


def _layout_query_weights(w_uq, w_abs):
    idx, sign = _rot_perm()
    w = w_uq.reshape(DEPTH, Q_RANK, MLA_H, MLA_DN + MLA_DR)
    pe = w[..., MLA_DN:]
    slot = jnp.concatenate([jnp.transpose(w_abs, (0, 2, 1, 3)), pe, pe[..., idx] * sign], axis=-1)
    return _pad_cols(slot, Q_SLOT).reshape(DEPTH, Q_RANK, MLA_H * Q_SLOT).astype(BF16)


def _layout_value_weights(w_ukv):
    w = w_ukv.reshape(DEPTH, KV_RANK, MLA_H, MLA_DN + MLA_DV)[..., MLA_DN:]
    eye = jnp.eye(MLA_H, dtype=w.dtype)
    bdw = jnp.einsum("lrhe,hg->lhrge", w, eye)
    return bdw.reshape(DEPTH, MLA_H * KV_RANK, MLA_H * MLA_DV).astype(BF16)


def _rope_tables(n):
    n_rows = n // GRID_W
    row = jnp.repeat(jnp.arange(n_rows, dtype=F32), GRID_W)
    col = jnp.tile(jnp.arange(GRID_W, dtype=F32), n_rows)
    half = MLA_DR // 2
    freqs = ROPE_BASE ** (-jnp.arange(half // 2, dtype=F32) * (2.0 / half))
    ar = row[:, None] * freqs
    ac = col[:, None] * freqs
    ang = jnp.concatenate([ar, ar, ac, ac], -1)
    return jnp.cos(ang), jnp.sin(ang)


def _block_diag_states(x):
    eye = jnp.eye(N_HEADS, dtype=x.dtype)
    y = jnp.einsum("...hde,hg->...hdge", x, eye)
    return y.reshape(x.shape[:-3] + (REC_W, REC_W))


def _diag_blocks(x):
    y = x.reshape(x.shape[:-2] + (N_HEADS, HEAD_D, N_HEADS, HEAD_D))
    return jnp.stack([y[..., h, :, h, :] for h in range(N_HEADS)], axis=-3)


def kernel(x_prompt, x_sample, cache_mla_ckv, cache_mla_kpe, state_mlstm_C, state_mlstm_n, state_mlstm_m, state_hgrn_S, c, c_ctx, w_mod, b_mod, w_in, b_in, mla_q_norm, w_uq, mla_kv_norm, w_ukv, mlstm_norm, hgrn_lb_logits, hgrn_norm, w_out, ln1_g, ln1_b, w_ffn_in, w_ffn_out, ln2_g, ln2_b):
    B, T, D = x_prompt.shape
    Bs, Ts, _ = x_sample.shape
    past = cache_mla_ckv.shape[2]
    consts = _recurrent_constants()

    cvec = jnp.concatenate([c_ctx[None, :], c, jnp.zeros((8 - 1 - Bs, D), F32)], axis=0)
    mod = jnp.transpose(_modulation(cvec, w_mod, b_mod), (0, 2, 1, 3))
    lbs = _hgrn_lower_bounds(hgrn_lb_logits)

    w_in_p, b_in_p = _layout_in_proj(w_in, b_in)
    wq4 = w_uq.reshape(DEPTH, Q_RANK, MLA_H, MLA_DN + MLA_DR)
    wkv4 = w_ukv.reshape(DEPTH, KV_RANK, MLA_H, MLA_DN + MLA_DV)
    w_abs = _absorbed_query_weights(jnp.transpose(wq4[..., :MLA_DN], (0, 2, 1, 3)),
                                    jnp.transpose(wkv4[..., :MLA_DN], (0, 2, 1, 3)))
    wq_p = _layout_query_weights(w_uq, w_abs)
    wuv_p = _layout_value_weights(w_ukv)
    w_out_b = w_out.astype(BF16)
    w_ffn_in_b = w_ffn_in.astype(BF16)
    w_ffn_out_b = w_ffn_out.astype(BF16)

    scale = (MLA_DN + MLA_DR) ** -0.5
    cos, sin = _rope_tables(Ts)
    qtab_s = scale * jnp.concatenate([jnp.ones((Ts, KV_RANK), F32), cos, sin, jnp.zeros((Ts, Q_SLOT - KV_RANK - 2 * MLA_DR), F32)], axis=1)
    ktab_s = jnp.concatenate([cos, sin, jnp.zeros((Ts, LANE - 2 * MLA_DR), F32)], axis=1)
    TM_C = 256
    lane_q = np.arange(Q_SLOT)
    qtab_c = jnp.asarray(np.tile(scale * (lane_q < KV_RANK + MLA_DR), (TM_C, 1)), F32)
    ktab_c = jnp.asarray(np.tile(np.arange(LANE) < MLA_DR, (TM_C, 1)), F32)

    kcache = jnp.concatenate([cache_mla_ckv, cache_mla_kpe, cache_mla_kpe,
                              jnp.zeros(cache_mla_kpe.shape[:-1] + (Q_SLOT - KV_RANK - 2 * MLA_DR,), F32)], axis=-1).astype(BF16)
    c0 = _block_diag_states(state_mlstm_C)
    n0 = state_mlstm_n.reshape(Bs, DEPTH, 2, 1, REC_W)
    m0 = _pad_cols(state_mlstm_m.reshape(Bs, DEPTH, 1, 2 * N_HEADS), LANE)
    s0 = _block_diag_states(jnp.swapaxes(state_hgrn_S, -1, -2))

    xp = x_prompt.reshape(B * T, D)
    xs = x_sample.reshape(Bs * Ts, D)
    nc_c, nc_s = T // CHUNK, Ts // CHUNK
    TM = 512
    ckv_l, kpe_l, fin_l = [], [], []
    for l in range(DEPTH):
        lb = lbs[l][None, :]
        mnorm, hnorm = mlstm_norm[l][None, :], hgrn_norm[l][None, :]
        qg, kg = mla_q_norm[l][None, :], mla_kv_norm[l][None, :]
        g1, b1, g2, b2 = ln1_g[l][None, :], ln1_b[l][None, :], ln2_g[l][None, :], ln2_b[l][None, :]
        mod_c, mod_s = mod[l, 0:1], mod[l, 1:1 + Bs]

        proj = _inproj(xp, mod_c, w_in_p[l], b_in_p[l], tm=TM, tiles_per_mod=B * T // TM)
        qa, ka, ckvn = _attn_prep(proj, qtab_c, ktab_c, wq_p[l], qg, kg, tm=TM_C, tab_tiles=1)
        o_lat = _attention(qa, ka, None, n_seq=B, seq_len=T, tq=T)
        scan = _state_scan(proj, lb, consts, None, n_seq=B, nc=nc_c, emit_final=True)
        mg = _mixer_outputs(proj, scan[:8], lb, mnorm, hnorm, consts)
        x1 = _outproj(xp, o_lat, mg, mod_c, wuv_p[l], w_out_b[l], g1, b1, tm=TM, tiles_per_mod=B * T // TM)
        xp = _ffn(x1, mod_c, w_ffn_in_b[l], w_ffn_out_b[l], g2, b2, tm=TM, tiles_per_mod=B * T // TM)
        ckv_l.append(ckvn.reshape(B, T, KV_RANK))
        kpe_l.append(proj[:, 3 * LANE:3 * LANE + MLA_DR].reshape(B, T, MLA_DR))
        fin_l.append(scan[8:])

        proj = _inproj(xs, mod_s, w_in_p[l], b_in_p[l], tm=TM, tiles_per_mod=Ts // TM)
        qa, ka, _ = _attn_prep(proj, qtab_s, ktab_s, wq_p[l], qg, kg, tm=TM, tab_tiles=Ts // TM)
        o_lat = _attention(qa, ka, kcache[:, l], n_seq=Bs, seq_len=Ts, tq=256)
        init = (c0[:, l], n0[:, l], m0[:, l], s0[:, l])
        scan = _state_scan(proj, lb, consts, init, n_seq=Bs, nc=nc_s, emit_final=False)
        mg = _mixer_outputs(proj, scan[:8], lb, mnorm, hnorm, consts)
        x1 = _outproj(xs, o_lat, mg, mod_s, wuv_p[l], w_out_b[l], g1, b1, tm=TM, tiles_per_mod=Ts // TM)
        xs = _ffn(x1, mod_s, w_ffn_in_b[l], w_ffn_out_b[l], g2, b2, tm=TM, tiles_per_mod=Ts // TM)

    new_ckv = jnp.stack(ckv_l, axis=1)
    new_kpe = jnp.stack(kpe_l, axis=1)
    new_c = jnp.stack([_diag_blocks(f[0]) for f in fin_l], axis=1)
    new_n = jnp.stack([f[1].reshape(B, 2, N_HEADS, HEAD_D) for f in fin_l], axis=1)
    new_m = jnp.stack([f[2][:, 0, :2 * N_HEADS].reshape(B, 2, N_HEADS) for f in fin_l], axis=1)
    new_s = jnp.stack([jnp.swapaxes(_diag_blocks(f[3]), -1, -2) for f in fin_l], axis=1)
    return (xp.reshape(B, T, D), xs.reshape(Bs, Ts, D), new_ckv, new_kpe, new_c, new_n, new_m, new_s)
```

```python
import functools

import numpy as np
import jax
import jax.numpy as jnp
from jax import lax
from jax.experimental import pallas as pl
from jax.experimental.pallas import tpu as pltpu

F32 = jnp.float32
BF16 = jnp.bfloat16

D_MODEL = 1024
DEPTH = 4
GRID_W = 64
MLA_H = 8
MLA_DN = 64
MLA_DR = 32
MLA_DV = 64
Q_RANK = 256
KV_RANK = 128
N_HEADS = 4
HEAD_D = 64
REC_W = N_HEADS * HEAD_D
FF = 2816
CHUNK = 64
ROPE_BASE = 10000.0
ALPHA = (2 * DEPTH) ** 0.25
EPS = 1e-6
TINY = 1e-30
NEG_BIG = -1e30
IN_SIZES = (256, 128, 32, 256, 256, 256, 256, 4, 4, 4, 4, 256, 256, 256, 256, 256)
LANE = 128
Q_SLOT = 2 * LANE
ONE_LANE = KV_RANK + 2 * MLA_DR
PROJ_W = 24 * LANE
VMEM_LIMIT = 56 * 1024 * 1024

BLK_CQ, BLK_KV, BLK_MQ, BLK_MK, BLK_MV, BLK_MO, BLK_GQ, BLK_GFF, BLK_GFB, BLK_GI, BLK_GG = range(11)
BLK_GATE_I, BLK_GATE_F = 22, 23


def _params(*sem):
    return pltpu.CompilerParams(dimension_semantics=sem, vmem_limit_bytes=VMEM_LIMIT)


def _split3(x):
    x1 = x.astype(BF16)
    r1 = x - x1.astype(F32)
    x2 = r1.astype(BF16)
    x3 = (r1 - x2.astype(F32)).astype(BF16)
    return x1, x2, x3


def _dot(a, b):
    return jnp.dot(a, b, preferred_element_type=F32)


def _dot_t(a, b):
    return lax.dot_general(a, b, (((1,), (1,)), ((), ())), preferred_element_type=F32)


def _sel_rows(sel3, x):
    return _dot(sel3, jnp.concatenate(_split3(x), axis=0))


def _sel_cols(x, sel3):
    return _dot(jnp.concatenate(_split3(x), axis=1), sel3)


def _seg_sum(x, bd2):
    x1 = x.astype(BF16)
    x2 = (x - x1.astype(F32)).astype(BF16)
    return _dot(jnp.concatenate([x1, x2], axis=1), bd2)


def _block_diag(x_bf16, bd):
    return jnp.concatenate([x_bf16] * N_HEADS, axis=0) * bd


def _log_sigmoid(x):
    return jnp.minimum(x, 0.0) - jnp.log1p(jnp.exp(-jnp.abs(x)))


def _silu(x):
    return x * jax.nn.sigmoid(x)


def _layernorm(y, g, b):
    mu = jnp.mean(y, axis=-1, keepdims=True)
    yc = y - mu
    var = jnp.mean(yc * yc, axis=-1, keepdims=True)
    return yc * lax.rsqrt(var + EPS) * g + b


def _recurrent_constants():
    L = CHUNK
    t = np.arange(L)
    tri = (t[None, :] <= t[:, None]).astype(np.float32)
    cum = np.stack([tri, tri.T])
    cum3 = np.tile(cum, (1, 1, 3))

    lane = np.arange(REC_W)
    exp = np.zeros((2, LANE, REC_W), np.float32)
    for d in range(2):
        exp[d, d * N_HEADS + lane // HEAD_D, lane] = 1.0
    exp3 = np.tile(exp, (1, 3, 1))

    bd = (lane[:, None] // HEAD_D == lane[None, :] // HEAD_D).astype(np.float32)
    bd2 = np.tile(bd, (2, 1))
    itile = (t[:, None] == (lane % HEAD_D)[None, :]).astype(np.float32)
    s_of_lane = lane % HEAD_D
    caus = np.stack([(s_of_lane[None, :] <= t[:, None]), (s_of_lane[None, :] >= t[:, None])]).astype(np.float32)

    def stage(G, g):
        ka = np.zeros((L, L), np.float32)
        qa = np.zeros((3, L, L), np.float32)
        msk = np.zeros((3, L, L), np.float32)
        for s in range(L):
            e = (s // g) * g + g - 1
            ka[s, s + 1:e + 1] = 1.0
        for tt in range(L):
            p = (tt % G) // g
            for j in range(3):
                if p > j:
                    E = (tt // G) * G + (j + 1) * g - 1
                    qa[j, tt, E + 1:tt + 1] = 1.0
                    for s in range(L):
                        if s // G == tt // G and (s % G) // g == j:
                            msk[j, tt, s] = 1.0
        return ka, qa, msk

    mats, msks = [], []
    for d in range(2):
        perm = t if d == 0 else t[::-1]

        def mir(m):
            return m[np.ix_(perm, perm)]

        ka16, qa16, m16 = stage(64, 16)
        ka4, qa4, m4 = stage(16, 4)
        _, qa1, m1 = stage(4, 1)
        blocks = [tri, ka16, qa16[0], qa16[1], qa16[2], ka4, qa4[0], qa4[1], qa4[2], qa1[0], qa1[1], qa1[2]]
        mats.append(np.concatenate([mir(b) for b in blocks], axis=0))
        mk = [m16[0], m16[1], m16[2], m4[0], m4[1], m4[2], m1[0], m1[1], m1[2], np.eye(L, dtype=np.float32)]
        msks.append(np.stack([np.tile(mir(m), (1, N_HEADS)) for m in mk]))
    mat3 = np.tile(np.stack(mats), (1, 1, 3))
    msk = np.stack(msks)
    return dict(
        cum3=jnp.asarray(cum3, BF16), exp3=jnp.asarray(exp3, BF16), bd=jnp.asarray(bd, BF16),
        bd2=jnp.asarray(bd2, BF16), itile=jnp.asarray(itile, F32), caus=jnp.asarray(caus, F32),
        mat3=jnp.asarray(mat3, BF16), msk=jnp.asarray(msk, F32))


def _mod_kernel(c_ref, w_ref, b_ref, o_ref):
    a = _silu(c_ref[...]).astype(BF16)
    o_ref[...] = _dot(a, w_ref[...].astype(BF16)) + b_ref[...]


def _modulation(cvec8, w_mod, b_mod):
    D = D_MODEL
    return pl.pallas_call(
        _mod_kernel,
        grid=(DEPTH, 6),
        in_specs=[
            pl.BlockSpec((8, D), lambda l, j: (0, 0)),
            pl.BlockSpec((None, D, D), lambda l, j: (l, 0, j)),
            pl.BlockSpec((None, None, 1, D), lambda l, j: (l, j, 0, 0)),
        ],
        out_specs=pl.BlockSpec((None, None, 8, D), lambda l, j: (l, j, 0, 0)),
        out_shape=jax.ShapeDtypeStruct((DEPTH, 6, 8, D), F32),
        compiler_params=_params("parallel", "parallel"),
        name="modulation",
    )(cvec8, w_mod, b_mod.reshape(DEPTH, 6, 1, D))


def _lb_kernel(x_ref, o_ref):
    x = x_ref[...]
    e = jnp.exp(x - jnp.max(x, axis=0, keepdims=True))
    p = e / jnp.sum(e, axis=0, keepdims=True)
    acc = jnp.zeros_like(p[0:1])
    rows = []
    for l in range(DEPTH):
        acc = acc + p[l:l + 1]
        rows.append(acc - p[0:1])
    o_ref[...] = jnp.concatenate(rows, axis=0)


def _hgrn_lower_bounds(logits):
    return pl.pallas_call(
        _lb_kernel, out_shape=jax.ShapeDtypeStruct(logits.shape, F32), name="hgrn_lower_bounds",
    )(logits.astype(F32))


def _absorb_kernel(a_ref, b_ref, o_ref):
    a1, a2, a3 = _split3(a_ref[...])
    b1, b2, b3 = _split3(b_ref[...])
    o_ref[...] = (_dot_t(a1, b1) + _dot_t(a1, b2) + _dot_t(a2, b1)
                  + _dot_t(a1, b3) + _dot_t(a2, b2) + _dot_t(a3, b1))


def _absorbed_query_weights(wq_nope, wk_nope):
    return pl.pallas_call(
        _absorb_kernel,
        grid=(DEPTH, MLA_H),
        in_specs=[
            pl.BlockSpec((None, None, Q_RANK, MLA_DN), lambda l, h: (l, h, 0, 0)),
            pl.BlockSpec((None, None, KV_RANK, MLA_DN), lambda l, h: (l, h, 0, 0)),
        ],
        out_specs=pl.BlockSpec((None, None, Q_RANK, KV_RANK), lambda l, h: (l, h, 0, 0)),
        out_shape=jax.ShapeDtypeStruct((DEPTH, MLA_H, Q_RANK, KV_RANK), F32),
        compiler_params=_params("parallel", "parallel"),
        name="absorb_query_weights",
    )(wq_nope, wk_nope)


def _inproj_kernel(x_ref, mod_ref, w_ref, b_ref, o_ref):
    sh, sc = mod_ref[0, 0:1, :], mod_ref[0, 1:2, :]
    h = (x_ref[...] * (1.0 + sc) + sh).astype(BF16)
    o_ref[...] = _dot(h, w_ref[...]) + b_ref[...]


def _inproj(x, mod, w, b, *, tm, tiles_per_mod):
    n = x.shape[0]
    return pl.pallas_call(
        _inproj_kernel,
        grid=(n // tm,),
        in_specs=[
            pl.BlockSpec((tm, D_MODEL), lambda i: (i, 0)),
            pl.BlockSpec((1, 6, D_MODEL), lambda i: (i // tiles_per_mod, 0, 0)),
            pl.BlockSpec((D_MODEL, PROJ_W), lambda i: (0, 0)),
            pl.BlockSpec((1, PROJ_W), lambda i: (0, 0)),
        ],
        out_specs=pl.BlockSpec((tm, PROJ_W), lambda i: (i, 0)),
        out_shape=jax.ShapeDtypeStruct((n, PROJ_W), F32),
        compiler_params=_params("parallel"),
        name="in_projection",
    )(x, mod, w, b)


def _attn_prep_kernel(cq_ref, kv_ref, qtab_ref, ktab_ref, wq_ref, qg_ref, kg_ref, q_ref, k_ref, ckv_ref):
    cq = cq_ref[...]
    cqn = cq * lax.rsqrt(jnp.mean(cq * cq, axis=-1, keepdims=True) + EPS) * qg_ref[...]
    qf = _dot(cqn.astype(BF16), wq_ref[...])
    qtab = qtab_ref[...]
    for h in range(MLA_H):
        q_ref[:, h * Q_SLOT:(h + 1) * Q_SLOT] = (qf[:, h * Q_SLOT:(h + 1) * Q_SLOT] * qtab).astype(BF16)
    ckv = kv_ref[:, :KV_RANK]
    ckvn = ckv * lax.rsqrt(jnp.mean(ckv * ckv, axis=-1, keepdims=True) + EPS) * kg_ref[...]
    ckv_ref[...] = ckvn
    k_ref[:, :KV_RANK] = ckvn.astype(BF16)
    t = kv_ref[:, KV_RANK:] * ktab_ref[...]
    kp = t + pltpu.roll(t, MLA_DR, 1) + pltpu.roll(t, LANE - MLA_DR, 1)
    lane = lax.broadcasted_iota(jnp.int32, kp.shape, 1)
    one = jnp.where(lane == ONE_LANE - KV_RANK, 1.0, 0.0)
    k_ref[:, KV_RANK:] = jnp.where(lane < 2 * MLA_DR, kp, one).astype(BF16)


def _attn_prep(proj, qtab, ktab, wq, qg, kg, *, tm, tab_tiles):
    n = proj.shape[0]
    return pl.pallas_call(
        _attn_prep_kernel,
        grid=(n // tm,),
        in_specs=[
            pl.BlockSpec((tm, Q_RANK), lambda i: (i, BLK_CQ)),
            pl.BlockSpec((tm, 2 * LANE), lambda i: (i, BLK_KV)),
            pl.BlockSpec((tm, Q_SLOT), lambda i: (i % tab_tiles, 0)),
            pl.BlockSpec((tm, LANE), lambda i: (i % tab_tiles, 0)),
            pl.BlockSpec((Q_RANK, MLA_H * Q_SLOT), lambda i: (0, 0)),
            pl.BlockSpec((1, Q_RANK), lambda i: (0, 0)),
            pl.BlockSpec((1, KV_RANK), lambda i: (0, 0)),
        ],
        out_specs=[
            pl.BlockSpec((tm, MLA_H * Q_SLOT), lambda i: (i, 0)),
            pl.BlockSpec((tm, Q_SLOT), lambda i: (i, 0)),
            pl.BlockSpec((tm, KV_RANK), lambda i: (i, 0)),
        ],
        out_shape=[
            jax.ShapeDtypeStruct((n, MLA_H * Q_SLOT), BF16),
            jax.ShapeDtypeStruct((n, Q_SLOT), BF16),
            jax.ShapeDtypeStruct((n, KV_RANK), F32),
        ],
        compiler_params=_params("parallel"),
        name="attention_prep",
    )(proj, proj, qtab, ktab, wq, qg, kg)


def _attn_kernel(*refs, has_cache, heads, tk, lockstep):
    if has_cache:
        q_ref, k_ref, kc_ref, o_ref = refs
    else:
        q_ref, k_ref, o_ref = refs
    n_new = k_ref.shape[0]
    blocks = [(k_ref, j * tk, min(tk, n_new - j * tk)) for j in range(pl.cdiv(n_new, tk))]
    if has_cache:
        n_old = kc_ref.shape[0]
        blocks += [(kc_ref, j * tk, min(tk, n_old - j * tk)) for j in range(pl.cdiv(n_old, tk))]
    tq = q_ref.shape[0]
    def head(h):
        q = q_ref[:, h * Q_SLOT:(h + 1) * Q_SLOT]
        m = jnp.full((tq, 1), NEG_BIG, F32)
        acc = jnp.zeros((tq, Q_SLOT), F32)
        for ref, start, size in blocks:
            kj = ref[start:start + size, :]
            s = _dot_t(q, kj)
            yield
            m_new = jnp.maximum(m, jnp.max(s, axis=-1, keepdims=True))
            acc = acc * jnp.exp2(m - m_new) + _dot(jnp.exp2(s - m_new).astype(BF16), kj)
            m = m_new
        o_ref[:, h * KV_RANK:(h + 1) * KV_RANK] = (acc[:, :KV_RANK] / acc[:, ONE_LANE:ONE_LANE + 1]).astype(BF16)

    for h0 in range(0, heads, lockstep):
        _round_robin([head(h) for h in range(h0, h0 + lockstep)])


def _attention(q, k, kcache, *, n_seq, seq_len, tq, heads, lockstep, tk=512):
    n = q.shape[0]
    qt = seq_len // tq
    in_specs = [
        pl.BlockSpec((tq, heads * Q_SLOT), lambda b, i, h: (b * qt + i, h)),
        pl.BlockSpec((seq_len, Q_SLOT), lambda b, i, h: (b, 0)),
    ]
    args = [q, k]
    if kcache is not None:
        in_specs.append(pl.BlockSpec((None, kcache.shape[1], Q_SLOT), lambda b, i, h: (b, 0, 0)))
        args.append(kcache)
    return pl.pallas_call(
        functools.partial(_attn_kernel, has_cache=kcache is not None, heads=heads, tk=tk, lockstep=lockstep),
        grid=(n_seq, qt, MLA_H // heads),
        in_specs=in_specs,
        out_specs=pl.BlockSpec((tq, heads * KV_RANK), lambda b, i, h: (b * qt + i, h)),
        out_shape=jax.ShapeDtypeStruct((n, MLA_H * KV_RANK), BF16),
        compiler_params=_params("parallel", "parallel", "parallel"),
        name="attention",
    )(*args)


def _outproj_kernel(x_ref, ol_ref, mg_ref, mod_ref, wuv_ref, wo_ref, g_ref, b_ref, o_ref):
    a = _dot(ol_ref[...], wuv_ref[...]).astype(BF16)
    mix = _dot(a, wo_ref[:MLA_H * MLA_DV, :]) + _dot(mg_ref[...], wo_ref[MLA_H * MLA_DV:, :])
    g1 = mod_ref[0, 2:3, :]
    y = ALPHA * x_ref[...] + g1 * mix
    o_ref[...] = _layernorm(y, g_ref[...], b_ref[...])


def _outproj(x, o_lat, mg, mod, wuv, wo, g, b, *, tm, tiles_per_mod):
    n = x.shape[0]
    D = D_MODEL
    return pl.pallas_call(
        _outproj_kernel,
        grid=(n // tm,),
        in_specs=[
            pl.BlockSpec((tm, D), lambda i: (i, 0)),
            pl.BlockSpec((tm, MLA_H * KV_RANK), lambda i: (i, 0)),
            pl.BlockSpec((tm, 2 * REC_W), lambda i: (i, 0)),
            pl.BlockSpec((1, 6, D), lambda i: (i // tiles_per_mod, 0, 0)),
            pl.BlockSpec((MLA_H * KV_RANK, MLA_H * MLA_DV), lambda i: (0, 0)),
            pl.BlockSpec((D, D), lambda i: (0, 0)),
            pl.BlockSpec((1, D), lambda i: (0, 0)),
            pl.BlockSpec((1, D), lambda i: (0, 0)),
        ],
        out_specs=pl.BlockSpec((tm, D), lambda i: (i, 0)),
        out_shape=jax.ShapeDtypeStruct((n, D), F32),
        compiler_params=_params("parallel"),
        name="out_projection",
    )(x, o_lat, mg, mod, wuv, wo, g, b)


FF_STEPS = 2
FF_BLK = FF // FF_STEPS


def _ffn_kernel(x_ref, mod_ref, wg_ref, wu_ref, wo_ref, g_ref, b_ref, o_ref, h_scr, acc_scr):
    j = pl.program_id(1)

    @pl.when(j == 0)
    def _():
        sh, sc = mod_ref[0, 3:4, :], mod_ref[0, 4:5, :]
        h_scr[...] = (x_ref[...] * (1.0 + sc) + sh).astype(BF16)
        acc_scr[...] = jnp.zeros_like(acc_scr)

    h = h_scr[...]
    act = (_silu(_dot(h, wg_ref[...])) * _dot(h, wu_ref[...])).astype(BF16)
    acc_scr[...] += _dot(act, wo_ref[...])

    @pl.when(j == FF_STEPS - 1)
    def _():
        g2 = mod_ref[0, 5:6, :]
        y = ALPHA * x_ref[...] + g2 * acc_scr[...]
        o_ref[...] = _layernorm(y, g_ref[...], b_ref[...])


def _ffn(x, mod, w_in, w_out, g, b, *, tm, tiles_per_mod):
    n = x.shape[0]
    D = D_MODEL
    return pl.pallas_call(
        _ffn_kernel,
        grid=(n // tm, FF_STEPS),
        in_specs=[
            pl.BlockSpec((tm, D), lambda i, j: (i, 0)),
            pl.BlockSpec((1, 6, D), lambda i, j: (i // tiles_per_mod, 0, 0)),
            pl.BlockSpec((D, FF_BLK), lambda i, j: (0, j)),
            pl.BlockSpec((D, FF_BLK), lambda i, j: (0, FF_STEPS + j)),
            pl.BlockSpec((FF_BLK, D), lambda i, j: (j, 0)),
            pl.BlockSpec((1, D), lambda i, j: (0, 0)),
            pl.BlockSpec((1, D), lambda i, j: (0, 0)),
        ],
        out_specs=pl.BlockSpec((tm, D), lambda i, j: (i, 0)),
        out_shape=jax.ShapeDtypeStruct((n, D), F32),
        scratch_shapes=[pltpu.VMEM((tm, D), BF16), pltpu.VMEM((tm, D), F32)],
        compiler_params=_params("parallel", "arbitrary"),
        name="ffn",
    )(x, mod, w_in, w_in, w_out, g, b)


def _gate_terms(gti_f, gtf_f, gti_b, gtf_b, cum3_ref):
    lane = lax.broadcasted_iota(jnp.int32, (CHUNK, LANE), 1)
    is_f = lane < N_HEADS
    ig = jnp.where(is_f, gti_f, gti_b)
    lf = _log_sigmoid(jnp.where(is_f, gtf_f, gtf_b))
    b = jnp.where(is_f, _sel_rows(cum3_ref[0], lf), _sel_rows(cum3_ref[1], lf))
    return is_f, ig, lf, b


def _hgrn_gates(fr, lb):
    f = lb + (1.0 - lb) * jax.nn.sigmoid(fr)
    kk = (1.0 - lb) * jax.nn.sigmoid(-fr)
    return kk, jnp.log(jnp.maximum(f, TINY))


def _scan_kernel(*refs, nc, group, zero_init, emit_final):
    it = iter(refs)
    mk = (next(it), next(it))
    mv = (next(it), next(it))
    gf = (next(it), next(it))
    gi = (next(it), next(it))
    gti = (next(it), next(it))
    gtf = (next(it), next(it))
    lb_ref, cum3_ref, exp3_ref, bd_ref = next(it), next(it), next(it), next(it)
    if not zero_init:
        c0_ref, n0_ref, m0_ref, s0_ref = next(it), next(it), next(it), next(it)
    cs = (next(it), next(it))
    ns = (next(it), next(it))
    ms = (next(it), next(it))
    ss = (next(it), next(it))
    if emit_final:
        cfin_ref, nfin_ref, mfin_ref, sfin_ref = next(it), next(it), next(it), next(it)
    c_scr, n_scr, m_scr, s_scr = next(it), next(it), next(it), next(it)

    pos = pl.program_id(1)

    @pl.when(pos == 0)
    def _():
        if zero_init:
            c_scr[...] = jnp.zeros_like(c_scr)
            n_scr[...] = jnp.zeros_like(n_scr)
            m_scr[...] = jnp.zeros_like(m_scr)
            s_scr[...] = jnp.zeros_like(s_scr)
        else:
            c_scr[...] = c0_ref[...]
            n_scr[...] = n0_ref[...]
            m_scr[...] = m0_ref[...]
            s_scr[...] = s0_ref[...]

    bd = bd_ref[...]
    lb = lb_ref[...]
    def seq_step(q):
        _, ig, lf, b = _gate_terms(gti[0][q], gtf[0][q], gti[1][q], gtf[1][q], cum3_ref)
        hg = [_hgrn_gates(gf[d][q], lb) for d in range(2)]
        bcs = [_sel_rows(cum3_ref[d], hg[d][1]) for d in range(2)]
        yield
        r = ig - b
        rmax = jnp.max(r, axis=0, keepdims=True)
        g = jnp.sum(lf, axis=0, keepdims=True)
        m = m_scr[q]
        mm = jnp.maximum(m, rmax)
        sc = jnp.exp(m - mm)
        w = jnp.exp(r - mm)
        ms[0][q] = m
        ms[1][q] = m
        m_scr[q] = g + mm
        wsc = jnp.concatenate([w, jnp.broadcast_to(sc, (16, LANE))], axis=0)
        wxs = [_sel_cols(wsc, exp3_ref[d]) for d in range(2)]
        s_olds, s_upds, gls = [], [], []
        for d in range(2):
            kk, lg = hg[d]
            gl = jnp.sum(lg, axis=0, keepdims=True)
            kd = kk * jnp.exp(gl - bcs[d])
            s_old = s_scr[q, d]
            ss[d][q] = s_old.astype(BF16) * bd
            s_olds.append(s_old)
            gls.append(gl)
            s_upds.append(_dot(gi[d][q].T.astype(BF16), kd.astype(BF16)))
        yield
        news = []
        for d in range(2):
            c_old = c_scr[q, d]
            n_old = n_scr[q, d]
            cs[d][q] = c_old.astype(BF16) * bd
            ns[d][q] = n_old
            scx = wxs[d][CHUNK:CHUNK + 1]
            kw = mk[d][q] * (HEAD_D ** -0.5) * wxs[d][:CHUNK]
            c_upd = _dot(kw.T.astype(BF16), mv[d][q].astype(BF16))
            n_new = n_old * scx + jnp.sum(kw, axis=0, keepdims=True)
            news.append((c_old, scx, c_upd, n_new))
        yield
        for d in range(2):
            c_old, scx, c_upd, n_new = news[d]
            c_new = c_old * scx + c_upd
            s_new = s_olds[d] * jnp.exp(gls[d]) + s_upds[d]
            c_scr[q, d] = c_new
            n_scr[q, d] = n_new
            s_scr[q, d] = s_new
            if emit_final:
                @pl.when(pos == nc - 1)
                def _():
                    cfin_ref[q, d] = c_new
                    nfin_ref[q, d] = n_new
                    sfin_ref[q, d] = s_new
        if emit_final:
            @pl.when(pos == nc - 1)
            def _():
                mfin_ref[q] = g + mm

    _round_robin([seq_step(q) for q in range(group)])


def _state_scan(proj, lb, consts, init, *, n_seq, nc, group, emit_final):
    zero_init = init is None
    p4 = proj.reshape(n_seq, nc, CHUNK, PROJ_W)

    def fwd(p):
        return p

    def bwd(p):
        return nc - 1 - p

    def blk(col, w, idx):
        return pl.BlockSpec((group, None, CHUNK, w), lambda s, p: (s, idx(p), 0, col))

    in_specs, args = [], []
    for col, w in ((BLK_MK, REC_W), (BLK_MV, REC_W)):
        for idx in (fwd, bwd):
            in_specs.append(blk(col, w, idx))
            args.append(p4)
    in_specs += [blk(BLK_GFF, REC_W, fwd), blk(BLK_GFB, REC_W, bwd)]
    args += [p4, p4]
    for col, w in ((BLK_GI, REC_W), (BLK_GATE_I, LANE), (BLK_GATE_F, LANE)):
        for idx in (fwd, bwd):
            in_specs.append(blk(col, w, idx))
            args.append(p4)
    in_specs += [
        pl.BlockSpec((1, REC_W), lambda s, p: (0, 0)),
        pl.BlockSpec((2, CHUNK, 3 * CHUNK), lambda s, p: (0, 0, 0)),
        pl.BlockSpec((2, 3 * LANE, REC_W), lambda s, p: (0, 0, 0)),
        pl.BlockSpec((REC_W, REC_W), lambda s, p: (0, 0)),
    ]
    args += [lb, consts["cum3"], consts["exp3"], consts["bd"]]
    state_shapes = ((2, REC_W, REC_W), (2, 1, REC_W), (1, LANE), (2, REC_W, REC_W))
    if not zero_init:
        for shape in state_shapes:
            in_specs.append(pl.BlockSpec((group,) + shape, lambda s, p, k=len(shape): (s,) + (0,) * k))
        args += list(init)

    out_specs, out_shape = [], []

    def add_out(shape, dtype):
        for idx in (fwd, bwd):
            out_specs.append(pl.BlockSpec((group, None) + shape, lambda s, p, idx=idx: (s, idx(p)) + (0,) * len(shape)))
            out_shape.append(jax.ShapeDtypeStruct((n_seq, nc) + shape, dtype))

    add_out((REC_W, REC_W), BF16)
    add_out((1, REC_W), F32)
    add_out((1, LANE), F32)
    add_out((REC_W, REC_W), BF16)
    if emit_final:
        for shape in state_shapes:
            out_specs.append(pl.BlockSpec((group,) + shape, lambda s, p, k=len(shape): (s,) + (0,) * k))
            out_shape.append(jax.ShapeDtypeStruct((n_seq,) + shape, F32))

    outs = pl.pallas_call(
        functools.partial(_scan_kernel, nc=nc, group=group, zero_init=zero_init, emit_final=emit_final),
        grid=(n_seq // group, nc),
        in_specs=in_specs,
        out_specs=out_specs,
        out_shape=out_shape,
        scratch_shapes=[pltpu.VMEM((group,) + shape, F32) for shape in state_shapes],
        compiler_params=_params("parallel", "arbitrary"),
        name="state_scan",
    )(*args)
    starts = [o.reshape((n_seq * nc,) + o.shape[2:]) for o in outs[:8]]
    return starts + list(outs[8:])


def _round_robin(gens):
    results = [None] * len(gens)
    live = list(enumerate(gens))
    while live:
        still = []
        for i, g in live:
            try:
                next(g)
                still.append((i, g))
            except StopIteration as e:
                results[i] = e.value
        live = still
    return results


def _mlstm_chunk(mq, mk, mv, mo, gti, gtf, csf, csb, nsf, nsb, msf, msb,
                 mnorm, cum3_ref, exp3_ref, bd, bd2, itile, caus_ref):
    inv_d = 1.0 / HEAD_D
    is_f, ig, lf, b = _gate_terms(gti, gtf, gti, gtf, cum3_ref)
    yield
    r = ig - b
    row = lax.broadcasted_iota(jnp.int32, (CHUNK, LANE), 0)
    cm_f = r
    cm_b = r
    sh = 1
    while sh < CHUNK:
        cm_f = jnp.maximum(cm_f, jnp.where(row >= sh, pltpu.roll(cm_f, sh, 0), NEG_BIG))
        cm_b = jnp.maximum(cm_b, jnp.where(row < CHUNK - sh, pltpu.roll(cm_b, CHUNK - sh, 0), NEG_BIG))
        sh *= 2
    m_row = jnp.where(is_f[0:1], msf, msb)
    big_m = jnp.maximum(m_row, jnp.where(is_f, cm_f, cm_b))
    ws = jnp.exp(m_row - big_m)
    stacked = jnp.concatenate([big_m, ws, b + big_m, r], axis=0)

    qb = mq.astype(BF16)
    k_bd = _block_diag((mk * (HEAD_D ** -0.5)).astype(BF16), bd)
    v_bd = _block_diag(mv.astype(BF16), bd)
    s = _dot_t(qb, k_bd)
    ones3 = jnp.ones((CHUNK, 3 * CHUNK), BF16)
    ex = [_sel_cols(stacked, exp3_ref[d]) for d in range(2)]
    yield
    r_bcast = [_sel_rows(ones3, ex[d][3 * CHUNK:] * itile) for d in range(2)]
    yield
    p = []
    for d in range(2):
        arg = jnp.where(caus_ref[d] > 0.0, r_bcast[d] - ex[d][:CHUNK], NEG_BIG)
        p.append(s * jnp.exp(arg))
    pv = _dot(jnp.concatenate(p, axis=0).astype(BF16), v_bd)
    qc = [_dot(qb, c_st) for c_st in (csf, csb)]
    sums = [(_seg_sum(mq * n_st, bd2), _seg_sum(p[d], bd2)) for d, n_st in enumerate((nsf, nsb))]
    yield
    hsum = None
    for d in range(2):
        wsx = ex[d][CHUNK:2 * CHUNK]
        num = wsx * qc[d] + pv[d * CHUNK:(d + 1) * CHUNK]
        den = wsx * sums[d][0] + sums[d][1]
        hd = num / jnp.maximum(jnp.abs(den), jnp.exp(-ex[d][2 * CHUNK:3 * CHUNK]))
        hsum = hd if hsum is None else hsum + hd
    mu = _seg_sum(hsum, bd2) * inv_d
    yield
    hc = hsum - mu
    var = _seg_sum(hc * hc, bd2) * inv_d
    yield
    return hc * lax.rsqrt(var + EPS) * mnorm * jax.nn.sigmoid(mo)


def _hgrn_chunk(gq, gff, gfb, gi, gg, ssf, ssb, lb, hnorm, bd, bd2, mat3_ref, msk_ref):
    inv_d = 1.0 / HEAD_D
    qh = _silu(gq)
    gv_bd = _block_diag(gi.astype(BF16), bd)
    gates = [_hgrn_gates(fr, lb) for fr in (gff, gfb)]
    args = [_sel_rows(mat3_ref[d], gates[d][1]) for d in range(2)]
    yield
    a_tot = None
    inter = None
    for d, s_st in enumerate((ssf, ssb)):
        kk = gates[d][0]
        e = jnp.exp(args[d])

        def eb(i):
            return e[i * CHUNK:(i + 1) * CHUNK]

        t_in = _dot_t((qh * eb(0)).astype(BF16), s_st)
        inter = t_in if inter is None else inter + t_in
        stages = (
            (kk * eb(1), [qh * eb(2), qh * eb(3), qh * eb(4)]),
            (kk * eb(5), [qh * eb(6), qh * eb(7), qh * eb(8)]),
            (kk, [qh * eb(9), qh * eb(10), qh * eb(11), qh]),
        )
        outs = [_dot_t(jnp.concatenate(q_list, axis=0).astype(BF16), _block_diag(k_st.astype(BF16), bd))
                for k_st, q_list in stages]
        yield
        mi = 0
        for out, (_, q_list) in zip(outs, stages):
            for j in range(len(q_list)):
                term = msk_ref[d, mi] * out[j * CHUNK:(j + 1) * CHUNK]
                a_tot = term if a_tot is None else a_tot + term
                mi += 1
    o = _dot(a_tot.astype(BF16), gv_bd) + inter
    yield
    ms = _seg_sum(o * o, bd2) * inv_d
    yield
    return o * lax.rsqrt(ms + EPS) * hnorm * _silu(gg)


def _mixer_kernel(*refs, chunks):
    mq, mk, mv, mo, gq, gff, gfb, gi, gg, gti, gtf = refs[:11]
    csf, csb, nsf, nsb, msf, msb, ssf, ssb = refs[11:19]
    lb_ref, mnorm_ref, hnorm_ref, cum3_ref, exp3_ref, bd_ref, bd2_ref, itile_ref, caus_ref, mat3_ref, msk_ref = refs[19:30]
    o_ref = refs[30]
    bd, bd2, itile = bd_ref[...], bd2_ref[...], itile_ref[...]
    lb, mnorm, hnorm = lb_ref[...], mnorm_ref[...], hnorm_ref[...]
    gens = []
    for c in range(chunks):
        rows = slice(c * CHUNK, (c + 1) * CHUNK)
        gens.append(_mlstm_chunk(*[r[rows, :] for r in (mq, mk, mv, mo, gti, gtf)],
                                 *[r[c] for r in (csf, csb, nsf, nsb, msf, msb)],
                                 mnorm, cum3_ref, exp3_ref, bd, bd2, itile, caus_ref))
        gens.append(_hgrn_chunk(*[r[rows, :] for r in (gq, gff, gfb, gi, gg)], ssf[c], ssb[c],
                                lb, hnorm, bd, bd2, mat3_ref, msk_ref))
    outs = _round_robin(gens)
    for c in range(chunks):
        rows = slice(c * CHUNK, (c + 1) * CHUNK)
        o_ref[rows, :REC_W] = outs[2 * c].astype(BF16)
        o_ref[rows, REC_W:] = outs[2 * c + 1].astype(BF16)


def _mixer_outputs(proj, states, lb, mnorm, hnorm, consts, *, chunks):
    n = proj.shape[0]
    rows = chunks * CHUNK

    def blk(col, w):
        return pl.BlockSpec((rows, w), lambda c: (c, col))

    in_specs = [blk(c, REC_W) for c in (BLK_MQ, BLK_MK, BLK_MV, BLK_MO, BLK_GQ, BLK_GFF, BLK_GFB, BLK_GI, BLK_GG)]
    in_specs += [blk(BLK_GATE_I, LANE), blk(BLK_GATE_F, LANE)]
    args = [proj] * 11
    for arr in states:
        shape = arr.shape[1:]
        in_specs.append(pl.BlockSpec((chunks,) + shape, lambda c, k=len(shape): (c,) + (0,) * k))
        args.append(arr)

    def whole(arr):
        return pl.BlockSpec(arr.shape, lambda c, k=arr.ndim: (0,) * k)

    for arr in (lb, mnorm, hnorm, consts["cum3"], consts["exp3"], consts["bd"], consts["bd2"], consts["itile"],
                consts["caus"], consts["mat3"], consts["msk"]):
        in_specs.append(whole(arr))
        args.append(arr)
    return pl.pallas_call(
        functools.partial(_mixer_kernel, chunks=chunks),
        grid=(n // rows,),
        in_specs=in_specs,
        out_specs=pl.BlockSpec((rows, 2 * REC_W), lambda c: (c, 0)),
        out_shape=jax.ShapeDtypeStruct((n, 2 * REC_W), BF16),
        compiler_params=_params("parallel"),
        name="mixer_outputs",
    )(*args)


def _rot_perm():
    q = MLA_DR // 4
    idx = np.concatenate([np.arange(q, 2 * q), np.arange(0, q), np.arange(3 * q, 4 * q), np.arange(2 * q, 3 * q)])
    sign = np.concatenate([-np.ones(q), np.ones(q), -np.ones(q), np.ones(q)]).astype(np.float32)
    return idx, sign


def _pad_cols(a, width):
    return jnp.pad(a, [(0, 0)] * (a.ndim - 1) + [(0, width - a.shape[-1])])


def _layout_in_proj(w_in, b_in):
    off = np.concatenate([[0], np.cumsum(IN_SIZES)])
    idx, sign = _rot_perm()

    def lay(a):
        def cols(i, j=None):
            return a[..., int(off[i]):int(off[(i if j is None else j) + 1])]

        kpe = cols(2)
        return jnp.concatenate([
            cols(0), cols(1),
            _pad_cols(jnp.concatenate([kpe, kpe[..., idx] * sign], axis=-1), LANE),
            cols(3, 6), cols(11, 15),
            _pad_cols(jnp.concatenate([cols(7), cols(8)], axis=-1), LANE),
            _pad_cols(jnp.concatenate([cols(9), cols(10)], axis=-1), LANE),
        ], axis=-1)

    return lay(w_in).astype(BF16), lay(b_in)[:, None, :]


def _layout_query_weights(w_uq, w_abs):
    idx, sign = _rot_perm()
    w = w_uq.reshape(DEPTH, Q_RANK, MLA_H, MLA_DN + MLA_DR)
    pe = w[..., MLA_DN:]
    slot = jnp.concatenate([jnp.transpose(w_abs, (0, 2, 1, 3)), pe, pe[..., idx] * sign], axis=-1)
    return _pad_cols(slot, Q_SLOT).reshape(DEPTH, Q_RANK, MLA_H * Q_SLOT).astype(BF16)


def _layout_value_weights(w_ukv):
    w = w_ukv.reshape(DEPTH, KV_RANK, MLA_H, MLA_DN + MLA_DV)[..., MLA_DN:]
    eye = jnp.eye(MLA_H, dtype=w.dtype)
    bdw = jnp.einsum("lrhe,hg->lhrge", w, eye)
    return bdw.reshape(DEPTH, MLA_H * KV_RANK, MLA_H * MLA_DV).astype(BF16)


def _rope_tables(n):
    n_rows = n // GRID_W
    row = jnp.repeat(jnp.arange(n_rows, dtype=F32), GRID_W)
    col = jnp.tile(jnp.arange(GRID_W, dtype=F32), n_rows)
    half = MLA_DR // 2
    freqs = ROPE_BASE ** (-jnp.arange(half // 2, dtype=F32) * (2.0 / half))
    ar = row[:, None] * freqs
    ac = col[:, None] * freqs
    ang = jnp.concatenate([ar, ar, ac, ac], -1)
    return jnp.cos(ang), jnp.sin(ang)


def _block_diag_states(x):
    eye = jnp.eye(N_HEADS, dtype=x.dtype)
    y = jnp.einsum("...hde,hg->...hdge", x, eye)
    return y.reshape(x.shape[:-3] + (REC_W, REC_W))


def _diag_blocks(x):
    y = x.reshape(x.shape[:-2] + (N_HEADS, HEAD_D, N_HEADS, HEAD_D))
    return jnp.stack([y[..., h, :, h, :] for h in range(N_HEADS)], axis=-3)


def kernel(x_prompt, x_sample, cache_mla_ckv, cache_mla_kpe, state_mlstm_C, state_mlstm_n, state_mlstm_m, state_hgrn_S, c, c_ctx, w_mod, b_mod, w_in, b_in, mla_q_norm, w_uq, mla_kv_norm, w_ukv, mlstm_norm, hgrn_lb_logits, hgrn_norm, w_out, ln1_g, ln1_b, w_ffn_in, w_ffn_out, ln2_g, ln2_b):
    B, T, D = x_prompt.shape
    Bs, Ts, _ = x_sample.shape
    past = cache_mla_ckv.shape[2]
    consts = _recurrent_constants()

    cvec = jnp.concatenate([c_ctx[None, :], c, jnp.zeros((8 - 1 - Bs, D), F32)], axis=0)
    mod = jnp.transpose(_modulation(cvec, w_mod, b_mod), (0, 2, 1, 3))
    lbs = _hgrn_lower_bounds(hgrn_lb_logits)

    w_in_p, b_in_p = _layout_in_proj(w_in, b_in)
    wq4 = w_uq.reshape(DEPTH, Q_RANK, MLA_H, MLA_DN + MLA_DR)
    wkv4 = w_ukv.reshape(DEPTH, KV_RANK, MLA_H, MLA_DN + MLA_DV)
    w_abs = _absorbed_query_weights(jnp.transpose(wq4[..., :MLA_DN], (0, 2, 1, 3)),
                                    jnp.transpose(wkv4[..., :MLA_DN], (0, 2, 1, 3)))
    wq_p = _layout_query_weights(w_uq, w_abs)
    wuv_p = _layout_value_weights(w_ukv)
    w_out_b = w_out.astype(BF16)
    w_ffn_in_b = w_ffn_in.astype(BF16)
    w_ffn_out_b = w_ffn_out.astype(BF16)

    scale = (MLA_DN + MLA_DR) ** -0.5 * float(np.log2(np.e))
    cos, sin = _rope_tables(Ts)
    qtab_s = scale * jnp.concatenate([jnp.ones((Ts, KV_RANK), F32), cos, sin, jnp.zeros((Ts, Q_SLOT - KV_RANK - 2 * MLA_DR), F32)], axis=1)
    ktab_s = jnp.concatenate([cos, sin, jnp.zeros((Ts, LANE - 2 * MLA_DR), F32)], axis=1)
    TM_C = 256
    lane_q = np.arange(Q_SLOT)
    qtab_c = jnp.asarray(np.tile(scale * (lane_q < KV_RANK + MLA_DR), (TM_C, 1)), F32)
    ktab_c = jnp.asarray(np.tile(np.arange(LANE) < MLA_DR, (TM_C, 1)), F32)

    kcache = jnp.concatenate([cache_mla_ckv, cache_mla_kpe, cache_mla_kpe,
                              jnp.ones(cache_mla_kpe.shape[:-1] + (1,), F32),
                              jnp.zeros(cache_mla_kpe.shape[:-1] + (Q_SLOT - ONE_LANE - 1,), F32)], axis=-1).astype(BF16)
    c0 = _block_diag_states(state_mlstm_C)
    n0 = state_mlstm_n.reshape(Bs, DEPTH, 2, 1, REC_W)
    m0 = _pad_cols(state_mlstm_m.reshape(Bs, DEPTH, 1, 2 * N_HEADS), LANE)
    s0 = _block_diag_states(jnp.swapaxes(state_hgrn_S, -1, -2))

    xp = x_prompt.reshape(B * T, D)
    xs = x_sample.reshape(Bs * Ts, D)
    nc_c, nc_s = T // CHUNK, Ts // CHUNK
    TM = 512
    ckv_l, kpe_l, fin_l = [], [], []
    for l in range(DEPTH):
        lb = lbs[l][None, :]
        mnorm, hnorm = mlstm_norm[l][None, :], hgrn_norm[l][None, :]
        qg, kg = mla_q_norm[l][None, :], mla_kv_norm[l][None, :]
        g1, b1, g2, b2 = ln1_g[l][None, :], ln1_b[l][None, :], ln2_g[l][None, :], ln2_b[l][None, :]
        mod_c, mod_s = mod[l, 0:1], mod[l, 1:1 + Bs]

        proj = _inproj(xp, mod_c, w_in_p[l], b_in_p[l], tm=TM, tiles_per_mod=B * T // TM)
        qa, ka, ckvn = _attn_prep(proj, qtab_c, ktab_c, wq_p[l], qg, kg, tm=TM_C, tab_tiles=1)
        o_lat = _attention(qa, ka, None, n_seq=B, seq_len=T, tq=T, heads=MLA_H, lockstep=MLA_H)
        scan = _state_scan(proj, lb, consts, None, n_seq=B, nc=nc_c, group=4, emit_final=True)
        mg = _mixer_outputs(proj, scan[:8], lb, mnorm, hnorm, consts, chunks=4)
        x1 = _outproj(xp, o_lat, mg, mod_c, wuv_p[l], w_out_b[l], g1, b1, tm=TM, tiles_per_mod=B * T // TM)
        xp = _ffn(x1, mod_c, w_ffn_in_b[l], w_ffn_out_b[l], g2, b2, tm=TM, tiles_per_mod=B * T // TM)
        ckv_l.append(ckvn.reshape(B, T, KV_RANK))
        kpe_l.append(proj[:, 3 * LANE:3 * LANE + MLA_DR].reshape(B, T, MLA_DR))
        fin_l.append(scan[8:])

        proj = _inproj(xs, mod_s, w_in_p[l], b_in_p[l], tm=TM, tiles_per_mod=Ts // TM)
        qa, ka, _ = _attn_prep(proj, qtab_s, ktab_s, wq_p[l], qg, kg, tm=TM, tab_tiles=Ts // TM)
        o_lat = _attention(qa, ka, kcache[:, l], n_seq=Bs, seq_len=Ts, tq=256, heads=8, lockstep=1)
        init = (c0[:, l], n0[:, l], m0[:, l], s0[:, l])
        scan = _state_scan(proj, lb, consts, init, n_seq=Bs, nc=nc_s, group=Bs, emit_final=False)
        mg = _mixer_outputs(proj, scan[:8], lb, mnorm, hnorm, consts, chunks=4)
        x1 = _outproj(xs, o_lat, mg, mod_s, wuv_p[l], w_out_b[l], g1, b1, tm=TM, tiles_per_mod=Ts // TM)
        xs = _ffn(x1, mod_s, w_ffn_in_b[l], w_ffn_out_b[l], g2, b2, tm=TM, tiles_per_mod=Ts // TM)

    new_ckv = jnp.stack(ckv_l, axis=1)
    new_kpe = jnp.stack(kpe_l, axis=1)
    new_c = jnp.stack([_diag_blocks(f[0]) for f in fin_l], axis=1)
    new_n = jnp.stack([f[1].reshape(B, 2, N_HEADS, HEAD_D) for f in fin_l], axis=1)
    new_m = jnp.stack([f[2][:, 0, :2 * N_HEADS].reshape(B, 2, N_HEADS) for f in fin_l], axis=1)
    new_s = jnp.stack([jnp.swapaxes(_diag_blocks(f[3]), -1, -2) for f in fin_l], axis=1)
    return (xp.reshape(B, T, D), xs.reshape(Bs, Ts, D), new_ckv, new_kpe, new_c, new_n, new_m, new_s)
```

```python
import functools

import numpy as np
import jax
import jax.numpy as jnp
from jax import lax
from jax.experimental import pallas as pl
from jax.experimental.pallas import tpu as pltpu

F32 = jnp.float32
BF16 = jnp.bfloat16

D_MODEL = 1024
DEPTH = 4
GRID_W = 64
MLA_H = 8
MLA_DN = 64
MLA_DR = 32
MLA_DV = 64
Q_RANK = 256
KV_RANK = 128
N_HEADS = 4
HEAD_D = 64
REC_W = N_HEADS * HEAD_D
FF = 2816
CHUNK = 64
ROPE_BASE = 10000.0
ALPHA = (2 * DEPTH) ** 0.25
EPS = 1e-6
TINY = 1e-30
NEG_BIG = -1e30
IN_SIZES = (256, 128, 32, 256, 256, 256, 256, 4, 4, 4, 4, 256, 256, 256, 256, 256)
LANE = 128
Q_SLOT = 2 * LANE
ONE_LANE = KV_RANK + 2 * MLA_DR
PROJ_W = 24 * LANE
VMEM_LIMIT = 56 * 1024 * 1024

BLK_CQ, BLK_KV, BLK_MQ, BLK_MK, BLK_MV, BLK_MO, BLK_GQ, BLK_GFF, BLK_GFB, BLK_GI, BLK_GG = range(11)
BLK_GATE_I, BLK_GATE_F = 22, 23


def _params(*sem):
    return pltpu.CompilerParams(dimension_semantics=sem, vmem_limit_bytes=VMEM_LIMIT)


def _split3(x):
    x1 = x.astype(BF16)
    r1 = x - x1.astype(F32)
    x2 = r1.astype(BF16)
    x3 = (r1 - x2.astype(F32)).astype(BF16)
    return x1, x2, x3


def _dot(a, b):
    return jnp.dot(a, b, preferred_element_type=F32)


def _dot_t(a, b):
    return lax.dot_general(a, b, (((1,), (1,)), ((), ())), preferred_element_type=F32)


def _sel_rows(sel3, x):
    return _dot(sel3, jnp.concatenate(_split3(x), axis=0))


def _expand_heads(x, expp):
    x1 = x.astype(BF16).astype(F32)
    r1 = x - x1
    x2 = r1.astype(BF16).astype(F32)
    x3 = (r1 - x2).astype(BF16).astype(F32)
    lane = lax.broadcasted_iota(jnp.int32, x.shape, 1)
    n = 2 * N_HEADS
    packed = jnp.where(lane < n, x1, jnp.where(lane < 2 * n, pltpu.roll(x2, n, 1), pltpu.roll(x3, 2 * n, 1)))
    return _dot(packed.astype(BF16), expp)


def _seg_sum(x, bd2):
    x1 = x.astype(BF16)
    x2 = (x - x1.astype(F32)).astype(BF16)
    return _dot(jnp.concatenate([x1, x2], axis=1), bd2)


def _block_diag(x_bf16, bd):
    return jnp.concatenate([x_bf16] * N_HEADS, axis=0) * bd


def _log_sigmoid(x):
    return jnp.minimum(x, 0.0) - jnp.log1p(jnp.exp(-jnp.abs(x)))


def _silu(x):
    return x * jax.nn.sigmoid(x)


def _layernorm(y, g, b):
    mu = jnp.mean(y, axis=-1, keepdims=True)
    yc = y - mu
    var = jnp.mean(yc * yc, axis=-1, keepdims=True)
    return yc * lax.rsqrt(var + EPS) * g + b


def _recurrent_constants():
    L = CHUNK
    t = np.arange(L)
    tri = (t[None, :] <= t[:, None]).astype(np.float32)
    cum = np.stack([tri, tri.T])
    cum3 = np.tile(cum, (1, 1, 3))

    lane = np.arange(REC_W)
    expp = np.zeros((2, LANE, REC_W), np.float32)
    for d in range(2):
        for j in range(3):
            expp[d, 2 * N_HEADS * j + d * N_HEADS + lane // HEAD_D, lane] = 1.0

    bd = (lane[:, None] // HEAD_D == lane[None, :] // HEAD_D).astype(np.float32)
    bd2 = np.tile(bd, (2, 1))
    itile = (t[:, None] == (lane % HEAD_D)[None, :]).astype(np.float32)
    s_of_lane = lane % HEAD_D
    caus = np.stack([(s_of_lane[None, :] <= t[:, None]), (s_of_lane[None, :] >= t[:, None])]).astype(np.float32)

    def stage(G, g):
        ka = np.zeros((L, L), np.float32)
        qa = np.zeros((3, L, L), np.float32)
        msk = np.zeros((3, L, L), np.float32)
        for s in range(L):
            e = (s // g) * g + g - 1
            ka[s, s + 1:e + 1] = 1.0
        for tt in range(L):
            p = (tt % G) // g
            for j in range(3):
                if p > j:
                    E = (tt // G) * G + (j + 1) * g - 1
                    qa[j, tt, E + 1:tt + 1] = 1.0
                    for s in range(L):
                        if s // G == tt // G and (s % G) // g == j:
                            msk[j, tt, s] = 1.0
        return ka, qa, msk

    mats, msks = [], []
    for d in range(2):
        perm = t if d == 0 else t[::-1]

        def mir(m):
            return m[np.ix_(perm, perm)]

        ka16, qa16, m16 = stage(64, 16)
        ka4, qa4, m4 = stage(16, 4)
        _, qa1, m1 = stage(4, 1)
        blocks = [tri, ka16, qa16[0], qa16[1], qa16[2], ka4, qa4[0], qa4[1], qa4[2], qa1[0], qa1[1], qa1[2]]
        mats.append(np.concatenate([mir(b) for b in blocks], axis=0))
        mk = [m16[0], m16[1], m16[2], m4[0], m4[1], m4[2], m1[0], m1[1], m1[2], np.eye(L, dtype=np.float32)]
        msks.append(np.stack([np.tile(mir(m), (1, N_HEADS)) for m in mk]))
    mat3 = np.tile(np.stack(mats), (1, 1, 3))
    msk = np.stack(msks)
    return dict(
        cum3=jnp.asarray(cum3, BF16), expp=jnp.asarray(expp, BF16),
        bd=jnp.asarray(bd, BF16),
        bd2=jnp.asarray(bd2, BF16), itile=jnp.asarray(itile, F32), caus=jnp.asarray(caus, F32),
        mat3=jnp.asarray(mat3, BF16), msk=jnp.asarray(msk, F32))


def _mod_kernel(c_ref, w_ref, b_ref, o_ref):
    a = _silu(c_ref[...]).astype(BF16)
    o_ref[...] = _dot(a, w_ref[...].astype(BF16)) + b_ref[...]


def _modulation(cvec8, w_mod, b_mod):
    D = D_MODEL
    return pl.pallas_call(
        _mod_kernel,
        grid=(DEPTH, 6),
        in_specs=[
            pl.BlockSpec((8, D), lambda l, j: (0, 0)),
            pl.BlockSpec((None, D, D), lambda l, j: (l, 0, j)),
            pl.BlockSpec((None, None, 1, D), lambda l, j: (l, j, 0, 0)),
        ],
        out_specs=pl.BlockSpec((None, None, 8, D), lambda l, j: (l, j, 0, 0)),
        out_shape=jax.ShapeDtypeStruct((DEPTH, 6, 8, D), F32),
        compiler_params=_params("parallel", "parallel"),
        name="modulation",
    )(cvec8, w_mod, b_mod.reshape(DEPTH, 6, 1, D))


def _lb_kernel(x_ref, o_ref):
    x = x_ref[...]
    e = jnp.exp(x - jnp.max(x, axis=0, keepdims=True))
    p = e / jnp.sum(e, axis=0, keepdims=True)
    acc = jnp.zeros_like(p[0:1])
    rows = []
    for l in range(DEPTH):
        acc = acc + p[l:l + 1]
        rows.append(acc - p[0:1])
    o_ref[...] = jnp.concatenate(rows, axis=0)


def _hgrn_lower_bounds(logits):
    return pl.pallas_call(
        _lb_kernel, out_shape=jax.ShapeDtypeStruct(logits.shape, F32), name="hgrn_lower_bounds",
    )(logits.astype(F32))


def _absorb_kernel(a_ref, b_ref, o_ref):
    a1, a2, a3 = _split3(a_ref[...])
    b1, b2, b3 = _split3(b_ref[...])
    o_ref[...] = (_dot_t(a1, b1) + _dot_t(a1, b2) + _dot_t(a2, b1)
                  + _dot_t(a1, b3) + _dot_t(a2, b2) + _dot_t(a3, b1))


def _absorbed_query_weights(wq_nope, wk_nope):
    return pl.pallas_call(
        _absorb_kernel,
        grid=(DEPTH, MLA_H),
        in_specs=[
            pl.BlockSpec((None, None, Q_RANK, MLA_DN), lambda l, h: (l, h, 0, 0)),
            pl.BlockSpec((None, None, KV_RANK, MLA_DN), lambda l, h: (l, h, 0, 0)),
        ],
        out_specs=pl.BlockSpec((None, None, Q_RANK, KV_RANK), lambda l, h: (l, h, 0, 0)),
        out_shape=jax.ShapeDtypeStruct((DEPTH, MLA_H, Q_RANK, KV_RANK), F32),
        compiler_params=_params("parallel", "parallel"),
        name="absorb_query_weights",
    )(wq_nope, wk_nope)


def _inproj_kernel(x_ref, mod_ref, w_ref, b_ref, o_ref):
    sh, sc = mod_ref[0, 0:1, :], mod_ref[0, 1:2, :]
    h = (x_ref[...] * (1.0 + sc) + sh).astype(BF16)
    o_ref[...] = _dot(h, w_ref[...]) + b_ref[...]


def _inproj(x, mod, w, b, *, tm, tiles_per_mod):
    n = x.shape[0]
    return pl.pallas_call(
        _inproj_kernel,
        grid=(n // tm,),
        in_specs=[
            pl.BlockSpec((tm, D_MODEL), lambda i: (i, 0)),
            pl.BlockSpec((1, 6, D_MODEL), lambda i: (i // tiles_per_mod, 0, 0)),
            pl.BlockSpec((D_MODEL, PROJ_W), lambda i: (0, 0)),
            pl.BlockSpec((1, PROJ_W), lambda i: (0, 0)),
        ],
        out_specs=pl.BlockSpec((tm, PROJ_W), lambda i: (i, 0)),
        out_shape=jax.ShapeDtypeStruct((n, PROJ_W), F32),
        compiler_params=_params("parallel"),
        name="in_projection",
    )(x, mod, w, b)


def _attn_prep_kernel(cq_ref, kv_ref, qtab_ref, ktab_ref, wq_ref, qg_ref, kg_ref, q_ref, k_ref, ckv_ref):
    cq = cq_ref[...]
    cqn = cq * lax.rsqrt(jnp.mean(cq * cq, axis=-1, keepdims=True) + EPS) * qg_ref[...]
    qf = _dot(cqn.astype(BF16), wq_ref[...])
    qtab = qtab_ref[...]
    for h in range(MLA_H):
        q_ref[:, h * Q_SLOT:(h + 1) * Q_SLOT] = (qf[:, h * Q_SLOT:(h + 1) * Q_SLOT] * qtab).astype(BF16)
    ckv = kv_ref[:, :KV_RANK]
    ckvn = ckv * lax.rsqrt(jnp.mean(ckv * ckv, axis=-1, keepdims=True) + EPS) * kg_ref[...]
    ckv_ref[...] = ckvn
    k_ref[:, :KV_RANK] = ckvn.astype(BF16)
    t = kv_ref[:, KV_RANK:] * ktab_ref[...]
    kp = t + pltpu.roll(t, MLA_DR, 1) + pltpu.roll(t, LANE - MLA_DR, 1)
    lane = lax.broadcasted_iota(jnp.int32, kp.shape, 1)
    one = jnp.where(lane == ONE_LANE - KV_RANK, 1.0, 0.0)
    k_ref[:, KV_RANK:] = jnp.where(lane < 2 * MLA_DR, kp, one).astype(BF16)


def _attn_prep(proj, qtab, ktab, wq, qg, kg, *, tm, tab_tiles):
    n = proj.shape[0]
    return pl.pallas_call(
        _attn_prep_kernel,
        grid=(n // tm,),
        in_specs=[
            pl.BlockSpec((tm, Q_RANK), lambda i: (i, BLK_CQ)),
            pl.BlockSpec((tm, 2 * LANE), lambda i: (i, BLK_KV)),
            pl.BlockSpec((tm, Q_SLOT), lambda i: (i % tab_tiles, 0)),
            pl.BlockSpec((tm, LANE), lambda i: (i % tab_tiles, 0)),
            pl.BlockSpec((Q_RANK, MLA_H * Q_SLOT), lambda i: (0, 0)),
            pl.BlockSpec((1, Q_RANK), lambda i: (0, 0)),
            pl.BlockSpec((1, KV_RANK), lambda i: (0, 0)),
        ],
        out_specs=[
            pl.BlockSpec((tm, MLA_H * Q_SLOT), lambda i: (i, 0)),
            pl.BlockSpec((tm, Q_SLOT), lambda i: (i, 0)),
            pl.BlockSpec((tm, KV_RANK), lambda i: (i, 0)),
        ],
        out_shape=[
            jax.ShapeDtypeStruct((n, MLA_H * Q_SLOT), BF16),
            jax.ShapeDtypeStruct((n, Q_SLOT), BF16),
            jax.ShapeDtypeStruct((n, KV_RANK), F32),
        ],
        compiler_params=_params("parallel"),
        name="attention_prep",
    )(proj, proj, qtab, ktab, wq, qg, kg)


def _attn_kernel(*refs, has_cache, heads, tk, lockstep):
    if has_cache:
        q_ref, k_ref, kc_ref, o_ref = refs
    else:
        q_ref, k_ref, o_ref = refs
    n_new = k_ref.shape[0]
    blocks = [(k_ref, j * tk, min(tk, n_new - j * tk)) for j in range(pl.cdiv(n_new, tk))]
    if has_cache:
        n_old = kc_ref.shape[0]
        blocks += [(kc_ref, j * tk, min(tk, n_old - j * tk)) for j in range(pl.cdiv(n_old, tk))]
    tq = q_ref.shape[0]
    def head(h):
        q = q_ref[:, h * Q_SLOT:(h + 1) * Q_SLOT]
        m = jnp.full((tq, 1), NEG_BIG, F32)
        acc = jnp.zeros((tq, Q_SLOT), F32)
        for ref, start, size in blocks:
            kj = ref[start:start + size, :]
            s = _dot_t(q, kj)
            yield
            m_new = jnp.maximum(m, jnp.max(s, axis=-1, keepdims=True))
            acc = acc * jnp.exp2(m - m_new) + _dot(jnp.exp2(s - m_new).astype(BF16), kj)
            m = m_new
        o_ref[:, h * KV_RANK:(h + 1) * KV_RANK] = (acc[:, :KV_RANK] / acc[:, ONE_LANE:ONE_LANE + 1]).astype(BF16)

    for h0 in range(0, heads, lockstep):
        _round_robin([head(h) for h in range(h0, h0 + lockstep)])


def _attention(q, k, kcache, *, n_seq, seq_len, tq, heads, lockstep, tk=512):
    n = q.shape[0]
    qt = seq_len // tq
    in_specs = [
        pl.BlockSpec((tq, heads * Q_SLOT), lambda b, i, h: (b * qt + i, h)),
        pl.BlockSpec((seq_len, Q_SLOT), lambda b, i, h: (b, 0)),
    ]
    args = [q, k]
    if kcache is not None:
        in_specs.append(pl.BlockSpec((None, kcache.shape[1], Q_SLOT), lambda b, i, h: (b, 0, 0)))
        args.append(kcache)
    return pl.pallas_call(
        functools.partial(_attn_kernel, has_cache=kcache is not None, heads=heads, tk=tk, lockstep=lockstep),
        grid=(n_seq, qt, MLA_H // heads),
        in_specs=in_specs,
        out_specs=pl.BlockSpec((tq, heads * KV_RANK), lambda b, i, h: (b * qt + i, h)),
        out_shape=jax.ShapeDtypeStruct((n, MLA_H * KV_RANK), BF16),
        compiler_params=_params("parallel", "parallel", "parallel"),
        name="attention",
    )(*args)


OUTPROJ_PARTS = 4


def _outproj_kernel(x_ref, ol_ref, mg_ref, mod_ref, wuv_ref, wo_ref, g_ref, b_ref, o_ref):
    g1 = mod_ref[0, 2:3, :]
    rows_per_part = x_ref.shape[0] // OUTPROJ_PARTS

    def part(r):
        rows = slice(r * rows_per_part, (r + 1) * rows_per_part)
        a = _dot(ol_ref[rows, :], wuv_ref[...]).astype(BF16)
        yield
        mix = _dot(a, wo_ref[:MLA_H * MLA_DV, :]) + _dot(mg_ref[rows, :], wo_ref[MLA_H * MLA_DV:, :])
        yield
        y = ALPHA * x_ref[rows, :] + g1 * mix
        o_ref[rows, :] = _layernorm(y, g_ref[...], b_ref[...])

    _round_robin([part(r) for r in range(OUTPROJ_PARTS)], skew=1)


def _outproj(x, o_lat, mg, mod, wuv, wo, g, b, *, tm, tiles_per_mod):
    n = x.shape[0]
    D = D_MODEL
    return pl.pallas_call(
        _outproj_kernel,
        grid=(n // tm,),
        in_specs=[
            pl.BlockSpec((tm, D), lambda i: (i, 0)),
            pl.BlockSpec((tm, MLA_H * KV_RANK), lambda i: (i, 0)),
            pl.BlockSpec((tm, 2 * REC_W), lambda i: (i, 0)),
            pl.BlockSpec((1, 6, D), lambda i: (i // tiles_per_mod, 0, 0)),
            pl.BlockSpec((MLA_H * KV_RANK, MLA_H * MLA_DV), lambda i: (0, 0)),
            pl.BlockSpec((D, D), lambda i: (0, 0)),
            pl.BlockSpec((1, D), lambda i: (0, 0)),
            pl.BlockSpec((1, D), lambda i: (0, 0)),
        ],
        out_specs=pl.BlockSpec((tm, D), lambda i: (i, 0)),
        out_shape=jax.ShapeDtypeStruct((n, D), F32),
        compiler_params=_params("parallel"),
        name="out_projection",
    )(x, o_lat, mg, mod, wuv, wo, g, b)


FF_SUB = 256
FFN_PARTS = 2
FFN_SKEW = 3


def _ffn_kernel(x_ref, mod_ref, wi_ref, wo_ref, g_ref, b_ref, o_ref):
    sh, sc, g2 = mod_ref[0, 3:4, :], mod_ref[0, 4:5, :], mod_ref[0, 5:6, :]
    rows_per_part = x_ref.shape[0] // FFN_PARTS

    def part(r):
        rows = slice(r * rows_per_part, (r + 1) * rows_per_part)
        x = x_ref[rows, :]
        h = (x * (1.0 + sc) + sh).astype(BF16)
        acc = None
        for c0 in range(0, FF, FF_SUB):
            yield
            act = (_silu(_dot(h, wi_ref[:, c0:c0 + FF_SUB])) * _dot(h, wi_ref[:, FF + c0:FF + c0 + FF_SUB])).astype(BF16)
            upd = _dot(act, wo_ref[c0:c0 + FF_SUB, :])
            acc = upd if acc is None else acc + upd
        yield
        o_ref[rows, :] = _layernorm(ALPHA * x + g2 * acc, g_ref[...], b_ref[...])

    _round_robin([part(r) for r in range(FFN_PARTS)], skew=FFN_SKEW)


def _ffn(x, mod, w_in, w_out, g, b, *, tm, tiles_per_mod):
    n = x.shape[0]
    D = D_MODEL
    return pl.pallas_call(
        _ffn_kernel,
        grid=(n // tm,),
        in_specs=[
            pl.BlockSpec((tm, D), lambda i: (i, 0)),
            pl.BlockSpec((1, 6, D), lambda i: (i // tiles_per_mod, 0, 0)),
            pl.BlockSpec((D, 2 * FF), lambda i: (0, 0), pipeline_mode=pl.Buffered(1)),
            pl.BlockSpec((FF, D), lambda i: (0, 0), pipeline_mode=pl.Buffered(1)),
            pl.BlockSpec((1, D), lambda i: (0, 0)),
            pl.BlockSpec((1, D), lambda i: (0, 0)),
        ],
        out_specs=pl.BlockSpec((tm, D), lambda i: (i, 0)),
        out_shape=jax.ShapeDtypeStruct((n, D), F32),
        compiler_params=_params("parallel"),
        name="ffn",
    )(x, mod, w_in, w_out, g, b)


def _gate_terms(gti_f, gtf_f, gti_b, gtf_b, cum3_ref):
    lane = lax.broadcasted_iota(jnp.int32, (CHUNK, LANE), 1)
    is_f = lane < N_HEADS
    ig = jnp.where(is_f, gti_f, gti_b)
    lf = _log_sigmoid(jnp.where(is_f, gtf_f, gtf_b))
    b = jnp.where(is_f, _sel_rows(cum3_ref[0], lf), _sel_rows(cum3_ref[1], lf))
    return is_f, ig, lf, b


def _hgrn_gates(fr, lb):
    f = lb + (1.0 - lb) * jax.nn.sigmoid(fr)
    kk = (1.0 - lb) * jax.nn.sigmoid(-fr)
    return kk, jnp.log(jnp.maximum(f, TINY))


def _scan_kernel(*refs, nc, group, zero_init, emit_final):
    it = iter(refs)
    mk = (next(it), next(it))
    mv = (next(it), next(it))
    gf = (next(it), next(it))
    gi = (next(it), next(it))
    gti = (next(it), next(it))
    gtf = (next(it), next(it))
    lb_ref, cum3_ref, expp_ref, bd_ref = next(it), next(it), next(it), next(it)
    if not zero_init:
        c0_ref, n0_ref, m0_ref, s0_ref = next(it), next(it), next(it), next(it)
    cs = (next(it), next(it))
    ns = (next(it), next(it))
    ms = (next(it), next(it))
    ss = (next(it), next(it))
    if emit_final:
        cfin_ref, nfin_ref, mfin_ref, sfin_ref = next(it), next(it), next(it), next(it)
    c_scr, n_scr, m_scr, s_scr = next(it), next(it), next(it), next(it)

    pos = pl.program_id(1)

    @pl.when(pos == 0)
    def _():
        if zero_init:
            c_scr[...] = jnp.zeros_like(c_scr)
            n_scr[...] = jnp.zeros_like(n_scr)
            m_scr[...] = jnp.zeros_like(m_scr)
            s_scr[...] = jnp.zeros_like(s_scr)
        else:
            c_scr[...] = c0_ref[...]
            n_scr[...] = n0_ref[...]
            m_scr[...] = m0_ref[...]
            s_scr[...] = s0_ref[...]

    bd = bd_ref[...]
    lb = lb_ref[...]
    def seq_step(q):
        _, ig, lf, b = _gate_terms(gti[0][q], gtf[0][q], gti[1][q], gtf[1][q], cum3_ref)
        hg = [_hgrn_gates(gf[d][q], lb) for d in range(2)]
        bcs = [_sel_rows(cum3_ref[d], hg[d][1]) for d in range(2)]
        yield
        r = ig - b
        rmax = jnp.max(r, axis=0, keepdims=True)
        g = jnp.sum(lf, axis=0, keepdims=True)
        m = m_scr[q]
        mm = jnp.maximum(m, rmax)
        sc = jnp.exp(m - mm)
        w = jnp.exp(r - mm)
        ms[0][q] = m
        ms[1][q] = m
        m_scr[q] = g + mm
        wsc = jnp.concatenate([w, jnp.broadcast_to(sc, (16, LANE))], axis=0)
        wxs = [_expand_heads(wsc, expp_ref[d]) for d in range(2)]
        s_olds, s_upds, gls = [], [], []
        for d in range(2):
            kk, lg = hg[d]
            gl = jnp.sum(lg, axis=0, keepdims=True)
            kd = kk * jnp.exp(gl - bcs[d])
            s_old = s_scr[q, d]
            ss[d][q] = s_old.astype(BF16) * bd
            s_olds.append(s_old)
            gls.append(gl)
            s_upds.append(_dot(gi[d][q].T.astype(BF16), kd.astype(BF16)))
        yield
        news = []
        for d in range(2):
            c_old = c_scr[q, d]
            n_old = n_scr[q, d]
            cs[d][q] = c_old.astype(BF16) * bd
            ns[d][q] = n_old
            scx = wxs[d][CHUNK:CHUNK + 1]
            kw = mk[d][q] * (HEAD_D ** -0.5) * wxs[d][:CHUNK]
            c_upd = _dot(kw.T.astype(BF16), mv[d][q].astype(BF16))
            n_new = n_old * scx + jnp.sum(kw, axis=0, keepdims=True)
            news.append((c_old, scx, c_upd, n_new))
        yield
        for d in range(2):
            c_old, scx, c_upd, n_new = news[d]
            c_new = c_old * scx + c_upd
            s_new = s_olds[d] * jnp.exp(gls[d]) + s_upds[d]
            c_scr[q, d] = c_new
            n_scr[q, d] = n_new
            s_scr[q, d] = s_new
            if emit_final:
                @pl.when(pos == nc - 1)
                def _():
                    cfin_ref[q, d] = c_new
                    nfin_ref[q, d] = n_new
                    sfin_ref[q, d] = s_new
        if emit_final:
            @pl.when(pos == nc - 1)
            def _():
                mfin_ref[q] = g + mm

    _round_robin([seq_step(q) for q in range(group)])


def _state_scan(proj, lb, consts, init, *, n_seq, nc, group, emit_final):
    zero_init = init is None
    p4 = proj.reshape(n_seq, nc, CHUNK, PROJ_W)

    def fwd(p):
        return p

    def bwd(p):
        return nc - 1 - p

    def blk(col, w, idx):
        return pl.BlockSpec((group, None, CHUNK, w), lambda s, p: (s, idx(p), 0, col))

    in_specs, args = [], []
    for col, w in ((BLK_MK, REC_W), (BLK_MV, REC_W)):
        for idx in (fwd, bwd):
            in_specs.append(blk(col, w, idx))
            args.append(p4)
    in_specs += [blk(BLK_GFF, REC_W, fwd), blk(BLK_GFB, REC_W, bwd)]
    args += [p4, p4]
    for col, w in ((BLK_GI, REC_W), (BLK_GATE_I, LANE), (BLK_GATE_F, LANE)):
        for idx in (fwd, bwd):
            in_specs.append(blk(col, w, idx))
            args.append(p4)
    in_specs += [
        pl.BlockSpec((1, REC_W), lambda s, p: (0, 0)),
        pl.BlockSpec((2, CHUNK, 3 * CHUNK), lambda s, p: (0, 0, 0)),
        pl.BlockSpec((2, LANE, REC_W), lambda s, p: (0, 0, 0)),
        pl.BlockSpec((REC_W, REC_W), lambda s, p: (0, 0)),
    ]
    args += [lb, consts["cum3"], consts["expp"], consts["bd"]]
    state_shapes = ((2, REC_W, REC_W), (2, 1, REC_W), (1, LANE), (2, REC_W, REC_W))
    if not zero_init:
        for shape in state_shapes:
            in_specs.append(pl.BlockSpec((group,) + shape, lambda s, p, k=len(shape): (s,) + (0,) * k))
        args += list(init)

    out_specs, out_shape = [], []

    def add_out(shape, dtype):
        for idx in (fwd, bwd):
            out_specs.append(pl.BlockSpec((group, None) + shape, lambda s, p, idx=idx: (s, idx(p)) + (0,) * len(shape)))
            out_shape.append(jax.ShapeDtypeStruct((n_seq, nc) + shape, dtype))

    add_out((REC_W, REC_W), BF16)
    add_out((1, REC_W), F32)
    add_out((1, LANE), F32)
    add_out((REC_W, REC_W), BF16)
    if emit_final:
        for shape in state_shapes:
            out_specs.append(pl.BlockSpec((group,) + shape, lambda s, p, k=len(shape): (s,) + (0,) * k))
            out_shape.append(jax.ShapeDtypeStruct((n_seq,) + shape, F32))

    outs = pl.pallas_call(
        functools.partial(_scan_kernel, nc=nc, group=group, zero_init=zero_init, emit_final=emit_final),
        grid=(n_seq // group, nc),
        in_specs=in_specs,
        out_specs=out_specs,
        out_shape=out_shape,
        scratch_shapes=[pltpu.VMEM((group,) + shape, F32) for shape in state_shapes],
        compiler_params=_params("parallel", "arbitrary"),
        name="state_scan",
    )(*args)
    return list(outs)


def _round_robin(gens, skew=0):
    results = [None] * len(gens)
    done = [False] * len(gens)
    rnd = 0
    while not all(done):
        for i, g in enumerate(gens):
            if done[i] or rnd < skew * i:
                continue
            try:
                next(g)
            except StopIteration as e:
                results[i] = e.value
                done[i] = True
        rnd += 1
    return results


def _mlstm_chunk(mq, mk, mv, mo, gti, gtf, csf, csb, nsf, nsb, msf, msb,
                 mnorm, cum3_ref, expp_ref, bd, bd2, itile, caus_ref):
    inv_d = 1.0 / HEAD_D
    is_f, ig, lf, b = _gate_terms(gti, gtf, gti, gtf, cum3_ref)
    yield
    r = ig - b
    row = lax.broadcasted_iota(jnp.int32, (CHUNK, LANE), 0)
    cm_f = r
    cm_b = r
    sh = 1
    while sh < CHUNK:
        cm_f = jnp.maximum(cm_f, jnp.where(row >= sh, pltpu.roll(cm_f, sh, 0), NEG_BIG))
        cm_b = jnp.maximum(cm_b, jnp.where(row < CHUNK - sh, pltpu.roll(cm_b, CHUNK - sh, 0), NEG_BIG))
        sh *= 2
    m_row = jnp.where(is_f[0:1], msf, msb)
    big_m = jnp.maximum(m_row, jnp.where(is_f, cm_f, cm_b))
    ws = jnp.exp(m_row - big_m)
    stacked = jnp.concatenate([big_m, ws, b + big_m, r], axis=0)

    qb = mq.astype(BF16)
    k_bd = _block_diag((mk * (HEAD_D ** -0.5)).astype(BF16), bd)
    v_bd = _block_diag(mv.astype(BF16), bd)
    s = _dot_t(qb, k_bd)
    ones3 = jnp.ones((CHUNK, 3 * CHUNK), BF16)
    ex = [_expand_heads(stacked, expp_ref[d]) for d in range(2)]
    yield
    r_bcast = [_sel_rows(ones3, ex[d][3 * CHUNK:] * itile) for d in range(2)]
    yield
    p = []
    for d in range(2):
        arg = jnp.where(caus_ref[d] > 0.0, r_bcast[d] - ex[d][:CHUNK], NEG_BIG)
        p.append(s * jnp.exp(arg))
    pv = _dot(jnp.concatenate(p, axis=0).astype(BF16), v_bd)
    qc = [_dot(qb, c_st) for c_st in (csf, csb)]
    all_sums = _seg_sum(jnp.concatenate([mq * nsf, p[0], mq * nsb, p[1]], axis=0), bd2)
    sums = [(all_sums[2 * d * CHUNK:(2 * d + 1) * CHUNK], all_sums[(2 * d + 1) * CHUNK:(2 * d + 2) * CHUNK])
            for d in range(2)]
    yield
    hsum = None
    for d in range(2):
        wsx = ex[d][CHUNK:2 * CHUNK]
        num = wsx * qc[d] + pv[d * CHUNK:(d + 1) * CHUNK]
        den = wsx * sums[d][0] + sums[d][1]
        hd = num / jnp.maximum(jnp.abs(den), jnp.exp(-ex[d][2 * CHUNK:3 * CHUNK]))
        hsum = hd if hsum is None else hsum + hd
    mu = _seg_sum(hsum, bd2) * inv_d
    yield
    hc = hsum - mu
    var = _seg_sum(hc * hc, bd2) * inv_d
    yield
    return hc * lax.rsqrt(var + EPS) * mnorm * jax.nn.sigmoid(mo)


def _hgrn_chunk(gq, gff, gfb, gi, gg, ssf, ssb, lb, hnorm, bd, bd2, mat3_ref, msk_ref):
    inv_d = 1.0 / HEAD_D
    qh = _silu(gq)
    gv_bd = _block_diag(gi.astype(BF16), bd)
    gates = [_hgrn_gates(fr, lb) for fr in (gff, gfb)]
    args = [_sel_rows(mat3_ref[d], gates[d][1]) for d in range(2)]
    yield
    a_tot = None
    inter = None
    for d, s_st in enumerate((ssf, ssb)):
        kk = gates[d][0]
        e = jnp.exp(args[d])

        def eb(i):
            return e[i * CHUNK:(i + 1) * CHUNK]

        t_in = _dot_t((qh * eb(0)).astype(BF16), s_st)
        inter = t_in if inter is None else inter + t_in
        stages = (
            (kk * eb(1), [qh * eb(2), qh * eb(3), qh * eb(4)]),
            (kk * eb(5), [qh * eb(6), qh * eb(7), qh * eb(8)]),
            (kk, [qh * eb(9), qh * eb(10), qh * eb(11), qh]),
        )
        outs = [_dot_t(jnp.concatenate(q_list, axis=0).astype(BF16), _block_diag(k_st.astype(BF16), bd))
                for k_st, q_list in stages]
        yield
        mi = 0
        for out, (_, q_list) in zip(outs, stages):
            for j in range(len(q_list)):
                term = msk_ref[d, mi] * out[j * CHUNK:(j + 1) * CHUNK]
                a_tot = term if a_tot is None else a_tot + term
                mi += 1
    o = _dot(a_tot.astype(BF16), gv_bd) + inter
    yield
    ms = _seg_sum(o * o, bd2) * inv_d
    yield
    return o * lax.rsqrt(ms + EPS) * hnorm * _silu(gg)


def _mixer_kernel(*refs, chunks):
    mq, mk, mv, mo, gq, gff, gfb, gi, gg, gti, gtf = refs[:11]
    csf, csb, nsf, nsb, msf, msb, ssf, ssb = refs[11:19]
    lb_ref, mnorm_ref, hnorm_ref, cum3_ref, expp_ref, bd_ref, bd2_ref, itile_ref, caus_ref, mat3_ref, msk_ref = refs[19:30]
    o_ref = refs[30]
    bd, bd2, itile = bd_ref[...], bd2_ref[...], itile_ref[...]
    lb, mnorm, hnorm = lb_ref[...], mnorm_ref[...], hnorm_ref[...]
    gens = []
    for c in range(chunks):
        rows = slice(c * CHUNK, (c + 1) * CHUNK)
        gens.append(_mlstm_chunk(*[r[rows, :] for r in (mq, mk, mv, mo, gti, gtf)],
                                 *[r[c] for r in (csf, csb, nsf, nsb, msf, msb)],
                                 mnorm, cum3_ref, expp_ref, bd, bd2, itile, caus_ref))
        gens.append(_hgrn_chunk(*[r[rows, :] for r in (gq, gff, gfb, gi, gg)], ssf[c], ssb[c],
                                lb, hnorm, bd, bd2, mat3_ref, msk_ref))
    outs = _round_robin(gens)
    for c in range(chunks):
        rows = slice(c * CHUNK, (c + 1) * CHUNK)
        o_ref[rows, :REC_W] = outs[2 * c].astype(BF16)
        o_ref[rows, REC_W:] = outs[2 * c + 1].astype(BF16)


def _mixer_outputs(proj, states, lb, mnorm, hnorm, consts, *, chunks):
    n = proj.shape[0]
    rows = chunks * CHUNK

    def blk(col, w):
        return pl.BlockSpec((rows, w), lambda c: (c, col))

    in_specs = [blk(c, REC_W) for c in (BLK_MQ, BLK_MK, BLK_MV, BLK_MO, BLK_GQ, BLK_GFF, BLK_GFB, BLK_GI, BLK_GG)]
    in_specs += [blk(BLK_GATE_I, LANE), blk(BLK_GATE_F, LANE)]
    args = [proj] * 11
    for arr in states:
        shape = arr.shape[2:]
        per_seq = arr.shape[1] // chunks
        in_specs.append(pl.BlockSpec((None, chunks) + shape,
                                     lambda c, k=len(shape), per_seq=per_seq: (c // per_seq, c % per_seq) + (0,) * k))
        args.append(arr)

    def whole(arr):
        return pl.BlockSpec(arr.shape, lambda c, k=arr.ndim: (0,) * k)

    for arr in (lb, mnorm, hnorm, consts["cum3"], consts["expp"], consts["bd"], consts["bd2"], consts["itile"],
                consts["caus"], consts["mat3"], consts["msk"]):
        in_specs.append(whole(arr))
        args.append(arr)
    return pl.pallas_call(
        functools.partial(_mixer_kernel, chunks=chunks),
        grid=(n // rows,),
        in_specs=in_specs,
        out_specs=pl.BlockSpec((rows, 2 * REC_W), lambda c: (c, 0)),
        out_shape=jax.ShapeDtypeStruct((n, 2 * REC_W), BF16),
        compiler_params=_params("parallel"),
        name="mixer_outputs",
    )(*args)


def _rot_perm():
    q = MLA_DR // 4
    idx = np.concatenate([np.arange(q, 2 * q), np.arange(0, q), np.arange(3 * q, 4 * q), np.arange(2 * q, 3 * q)])
    sign = np.concatenate([-np.ones(q), np.ones(q), -np.ones(q), np.ones(q)]).astype(np.float32)
    return idx, sign


def _pad_cols(a, width):
    return jnp.pad(a, [(0, 0)] * (a.ndim - 1) + [(0, width - a.shape[-1])])


def _layout_in_proj(w_in, b_in):
    off = np.concatenate([[0], np.cumsum(IN_SIZES)])
    idx, sign = _rot_perm()

    def lay(a):
        def cols(i, j=None):
            return a[..., int(off[i]):int(off[(i if j is None else j) + 1])]

        kpe = cols(2)
        return jnp.concatenate([
            cols(0), cols(1),
            _pad_cols(jnp.concatenate([kpe, kpe[..., idx] * sign], axis=-1), LANE),
            cols(3, 6), cols(11, 15),
            _pad_cols(jnp.concatenate([cols(7), cols(8)], axis=-1), LANE),
            _pad_cols(jnp.concatenate([cols(9), cols(10)], axis=-1), LANE),
        ], axis=-1)

    return lay(w_in).astype(BF16), lay(b_in)[:, None, :]


def _layout_query_weights(w_uq, w_abs):
    idx, sign = _rot_perm()
    w = w_uq.reshape(DEPTH, Q_RANK, MLA_H, MLA_DN + MLA_DR)
    pe = w[..., MLA_DN:]
    slot = jnp.concatenate([jnp.transpose(w_abs, (0, 2, 1, 3)), pe, pe[..., idx] * sign], axis=-1)
    return _pad_cols(slot, Q_SLOT).reshape(DEPTH, Q_RANK, MLA_H * Q_SLOT).astype(BF16)


def _layout_value_weights(w_ukv):
    w = w_ukv.reshape(DEPTH, KV_RANK, MLA_H, MLA_DN + MLA_DV)[..., MLA_DN:]
    eye = jnp.eye(MLA_H, dtype=w.dtype)
    bdw = jnp.einsum("lrhe,hg->lhrge", w, eye)
    return bdw.reshape(DEPTH, MLA_H * KV_RANK, MLA_H * MLA_DV).astype(BF16)


def _rope_tables(n):
    n_rows = n // GRID_W
    row = jnp.repeat(jnp.arange(n_rows, dtype=F32), GRID_W)
    col = jnp.tile(jnp.arange(GRID_W, dtype=F32), n_rows)
    half = MLA_DR // 2
    freqs = ROPE_BASE ** (-jnp.arange(half // 2, dtype=F32) * (2.0 / half))
    ar = row[:, None] * freqs
    ac = col[:, None] * freqs
    ang = jnp.concatenate([ar, ar, ac, ac], -1)
    return jnp.cos(ang), jnp.sin(ang)


def _block_diag_states(x):
    eye = jnp.eye(N_HEADS, dtype=x.dtype)
    y = jnp.einsum("...hde,hg->...hdge", x, eye)
    return y.reshape(x.shape[:-3] + (REC_W, REC_W))


def _diag_blocks(x):
    y = x.reshape(x.shape[:-2] + (N_HEADS, HEAD_D, N_HEADS, HEAD_D))
    return jnp.stack([y[..., h, :, h, :] for h in range(N_HEADS)], axis=-3)


def kernel(x_prompt, x_sample, cache_mla_ckv, cache_mla_kpe, state_mlstm_C, state_mlstm_n, state_mlstm_m, state_hgrn_S, c, c_ctx, w_mod, b_mod, w_in, b_in, mla_q_norm, w_uq, mla_kv_norm, w_ukv, mlstm_norm, hgrn_lb_logits, hgrn_norm, w_out, ln1_g, ln1_b, w_ffn_in, w_ffn_out, ln2_g, ln2_b):
    B, T, D = x_prompt.shape
    Bs, Ts, _ = x_sample.shape
    past = cache_mla_ckv.shape[2]
    consts = _recurrent_constants()

    cvec = jnp.concatenate([c_ctx[None, :], c, jnp.zeros((8 - 1 - Bs, D), F32)], axis=0)
    mod = jnp.transpose(_modulation(cvec, w_mod, b_mod), (0, 2, 1, 3))
    lbs = _hgrn_lower_bounds(hgrn_lb_logits)

    w_in_p, b_in_p = _layout_in_proj(w_in, b_in)
    wq4 = w_uq.reshape(DEPTH, Q_RANK, MLA_H, MLA_DN + MLA_DR)
    wkv4 = w_ukv.reshape(DEPTH, KV_RANK, MLA_H, MLA_DN + MLA_DV)
    w_abs = _absorbed_query_weights(jnp.transpose(wq4[..., :MLA_DN], (0, 2, 1, 3)),
                                    jnp.transpose(wkv4[..., :MLA_DN], (0, 2, 1, 3)))
    wq_p = _layout_query_weights(w_uq, w_abs)
    wuv_p = _layout_value_weights(w_ukv)
    w_out_b = w_out.astype(BF16)
    w_ffn_in_b = w_ffn_in.astype(BF16)
    w_ffn_out_b = w_ffn_out.astype(BF16)

    scale = (MLA_DN + MLA_DR) ** -0.5 * float(np.log2(np.e))
    cos, sin = _rope_tables(Ts)
    qtab_s = scale * jnp.concatenate([jnp.ones((Ts, KV_RANK), F32), cos, sin, jnp.zeros((Ts, Q_SLOT - KV_RANK - 2 * MLA_DR), F32)], axis=1)
    ktab_s = jnp.concatenate([cos, sin, jnp.zeros((Ts, LANE - 2 * MLA_DR), F32)], axis=1)
    TM_C = 256
    lane_q = np.arange(Q_SLOT)
    qtab_c = jnp.asarray(np.tile(scale * (lane_q < KV_RANK + MLA_DR), (TM_C, 1)), F32)
    ktab_c = jnp.asarray(np.tile(np.arange(LANE) < MLA_DR, (TM_C, 1)), F32)

    kcache = jnp.concatenate([cache_mla_ckv, cache_mla_kpe, cache_mla_kpe,
                              jnp.ones(cache_mla_kpe.shape[:-1] + (1,), F32),
                              jnp.zeros(cache_mla_kpe.shape[:-1] + (Q_SLOT - ONE_LANE - 1,), F32)], axis=-1).astype(BF16)
    c0 = _block_diag_states(state_mlstm_C)
    n0 = state_mlstm_n.reshape(Bs, DEPTH, 2, 1, REC_W)
    m0 = _pad_cols(state_mlstm_m.reshape(Bs, DEPTH, 1, 2 * N_HEADS), LANE)
    s0 = _block_diag_states(jnp.swapaxes(state_hgrn_S, -1, -2))

    xp = x_prompt.reshape(B * T, D)
    xs = x_sample.reshape(Bs * Ts, D)
    nc_c, nc_s = T // CHUNK, Ts // CHUNK
    TM = 512
    TM_F = 1024
    ckv_l, kpe_l, fin_l = [], [], []
    for l in range(DEPTH):
        lb = lbs[l][None, :]
        mnorm, hnorm = mlstm_norm[l][None, :], hgrn_norm[l][None, :]
        qg, kg = mla_q_norm[l][None, :], mla_kv_norm[l][None, :]
        g1, b1, g2, b2 = ln1_g[l][None, :], ln1_b[l][None, :], ln2_g[l][None, :], ln2_b[l][None, :]
        mod_c, mod_s = mod[l, 0:1], mod[l, 1:1 + Bs]

        proj = _inproj(xp, mod_c, w_in_p[l], b_in_p[l], tm=TM, tiles_per_mod=B * T // TM)
        qa, ka, ckvn = _attn_prep(proj, qtab_c, ktab_c, wq_p[l], qg, kg, tm=TM_C, tab_tiles=1)
        o_lat = _attention(qa, ka, None, n_seq=B, seq_len=T, tq=T, heads=MLA_H, lockstep=MLA_H)
        scan = _state_scan(proj, lb, consts, None, n_seq=B, nc=nc_c, group=4, emit_final=True)
        mg = _mixer_outputs(proj, scan[:8], lb, mnorm, hnorm, consts, chunks=4)
        x1 = _outproj(xp, o_lat, mg, mod_c, wuv_p[l], w_out_b[l], g1, b1, tm=TM_F, tiles_per_mod=B * T // TM_F)
        xp = _ffn(x1, mod_c, w_ffn_in_b[l], w_ffn_out_b[l], g2, b2, tm=TM_F, tiles_per_mod=B * T // TM_F)
        ckv_l.append(ckvn.reshape(B, T, KV_RANK))
        kpe_l.append(proj[:, 3 * LANE:3 * LANE + MLA_DR].reshape(B, T, MLA_DR))
        fin_l.append(scan[8:])

        proj = _inproj(xs, mod_s, w_in_p[l], b_in_p[l], tm=TM, tiles_per_mod=Ts // TM)
        qa, ka, _ = _attn_prep(proj, qtab_s, ktab_s, wq_p[l], qg, kg, tm=TM, tab_tiles=Ts // TM)
        o_lat = _attention(qa, ka, kcache[:, l], n_seq=Bs, seq_len=Ts, tq=256, heads=8, lockstep=1)
        init = (c0[:, l], n0[:, l], m0[:, l], s0[:, l])
        scan = _state_scan(proj, lb, consts, init, n_seq=Bs, nc=nc_s, group=Bs, emit_final=False)
        mg = _mixer_outputs(proj, scan[:8], lb, mnorm, hnorm, consts, chunks=4)
        x1 = _outproj(xs, o_lat, mg, mod_s, wuv_p[l], w_out_b[l], g1, b1, tm=TM_F, tiles_per_mod=Ts // TM_F)
        xs = _ffn(x1, mod_s, w_ffn_in_b[l], w_ffn_out_b[l], g2, b2, tm=TM_F, tiles_per_mod=Ts // TM_F)

    new_ckv = jnp.stack(ckv_l, axis=1)
    new_kpe = jnp.stack(kpe_l, axis=1)
    new_c = jnp.stack([_diag_blocks(f[0]) for f in fin_l], axis=1)
    new_n = jnp.stack([f[1].reshape(B, 2, N_HEADS, HEAD_D) for f in fin_l], axis=1)
    new_m = jnp.stack([f[2][:, 0, :2 * N_HEADS].reshape(B, 2, N_HEADS) for f in fin_l], axis=1)
    new_s = jnp.stack([jnp.swapaxes(_diag_blocks(f[3]), -1, -2) for f in fin_l], axis=1)
    return (xp.reshape(B, T, D), xs.reshape(Bs, Ts, D), new_ckv, new_kpe, new_c, new_n, new_m, new_s)
```

```python
import functools

import numpy as np
import jax
import jax.numpy as jnp
from jax import lax
from jax.experimental import pallas as pl
from jax.experimental.pallas import tpu as pltpu

F32 = jnp.float32
BF16 = jnp.bfloat16

D_MODEL = 1024
DEPTH = 4
GRID_W = 64
MLA_H = 8
MLA_DN = 64
MLA_DR = 32
MLA_DV = 64
Q_RANK = 256
KV_RANK = 128
N_HEADS = 4
HEAD_D = 64
REC_W = N_HEADS * HEAD_D
FF = 2816
CHUNK = 64
ROPE_BASE = 10000.0
ALPHA = (2 * DEPTH) ** 0.25
EPS = 1e-6
TINY = 1e-30
NEG_BIG = -1e30
IN_SIZES = (256, 128, 32, 256, 256, 256, 256, 4, 4, 4, 4, 256, 256, 256, 256, 256)
LANE = 128
Q_SLOT = 2 * LANE
ONE_LANE = KV_RANK + 2 * MLA_DR
PROJ_W = 24 * LANE
VMEM_LIMIT = 56 * 1024 * 1024

BLK_CQ, BLK_KV, BLK_MQ, BLK_MK, BLK_MV, BLK_MO, BLK_GQ, BLK_GFF, BLK_GFB, BLK_GI, BLK_GG = range(11)
BLK_GATE_I, BLK_GATE_F = 22, 23


def _params(*sem):
    return pltpu.CompilerParams(dimension_semantics=sem, vmem_limit_bytes=VMEM_LIMIT)


def _split3(x):
    x1 = x.astype(BF16)
    r1 = x - x1.astype(F32)
    x2 = r1.astype(BF16)
    x3 = (r1 - x2.astype(F32)).astype(BF16)
    return x1, x2, x3


def _dot(a, b):
    return jnp.dot(a, b, preferred_element_type=F32)


def _dot_t(a, b):
    return lax.dot_general(a, b, (((1,), (1,)), ((), ())), preferred_element_type=F32)


def _sel_rows(sel3, x):
    return _dot(sel3, jnp.concatenate(_split3(x), axis=0))


def _expand_heads(x, expp):
    x1 = x.astype(BF16).astype(F32)
    r1 = x - x1
    x2 = r1.astype(BF16).astype(F32)
    x3 = (r1 - x2).astype(BF16).astype(F32)
    lane = lax.broadcasted_iota(jnp.int32, x.shape, 1)
    n = 2 * N_HEADS
    packed = jnp.where(lane < n, x1, jnp.where(lane < 2 * n, pltpu.roll(x2, n, 1), pltpu.roll(x3, 2 * n, 1)))
    return _dot(packed.astype(BF16), expp)


def _seg_sum(x, bd2):
    x1 = x.astype(BF16)
    x2 = (x - x1.astype(F32)).astype(BF16)
    return _dot(jnp.concatenate([x1, x2], axis=1), bd2)


def _block_diag(x_bf16, bd):
    return jnp.concatenate([x_bf16] * N_HEADS, axis=0) * bd


def _log_sigmoid(x):
    return jnp.minimum(x, 0.0) - jnp.log1p(jnp.exp(-jnp.abs(x)))


def _silu(x):
    return x * jax.nn.sigmoid(x)


def _layernorm(y, g, b):
    mu = jnp.mean(y, axis=-1, keepdims=True)
    yc = y - mu
    var = jnp.mean(yc * yc, axis=-1, keepdims=True)
    return yc * lax.rsqrt(var + EPS) * g + b


def _recurrent_constants():
    L = CHUNK
    t = np.arange(L)
    tri = (t[None, :] <= t[:, None]).astype(np.float32)
    cum = np.stack([tri, tri.T])
    cum3 = np.tile(cum, (1, 1, 3))

    lane = np.arange(REC_W)
    expp = np.zeros((2, LANE, REC_W), np.float32)
    for d in range(2):
        for j in range(3):
            expp[d, 2 * N_HEADS * j + d * N_HEADS + lane // HEAD_D, lane] = 1.0

    bd = (lane[:, None] // HEAD_D == lane[None, :] // HEAD_D).astype(np.float32)
    bd2 = np.tile(bd, (2, 1))
    itile = (t[:, None] == (lane % HEAD_D)[None, :]).astype(np.float32)
    s_of_lane = lane % HEAD_D
    caus = np.stack([(s_of_lane[None, :] <= t[:, None]), (s_of_lane[None, :] >= t[:, None])]).astype(np.float32)

    def stage(G, g):
        ka = np.zeros((L, L), np.float32)
        qa = np.zeros((3, L, L), np.float32)
        msk = np.zeros((3, L, L), np.float32)
        for s in range(L):
            e = (s // g) * g + g - 1
            ka[s, s + 1:e + 1] = 1.0
        for tt in range(L):
            p = (tt % G) // g
            for j in range(3):
                if p > j:
                    E = (tt // G) * G + (j + 1) * g - 1
                    qa[j, tt, E + 1:tt + 1] = 1.0
                    for s in range(L):
                        if s // G == tt // G and (s % G) // g == j:
                            msk[j, tt, s] = 1.0
        return ka, qa, msk

    mats, msks = [], []
    for d in range(2):
        perm = t if d == 0 else t[::-1]

        def mir(m):
            return m[np.ix_(perm, perm)]

        ka16, qa16, m16 = stage(64, 16)
        ka4, qa4, m4 = stage(16, 4)
        _, qa1, m1 = stage(4, 1)
        blocks = [tri, ka16, qa16[0], qa16[1], qa16[2], ka4, qa4[0], qa4[1], qa4[2], qa1[0], qa1[1], qa1[2]]
        mats.append(np.concatenate([mir(b) for b in blocks], axis=0))
        mk = [m16[0], m16[1], m16[2], m4[0], m4[1], m4[2], m1[0], m1[1], m1[2], np.eye(L, dtype=np.float32)]
        msks.append(np.stack([np.tile(mir(m), (1, N_HEADS)) for m in mk]))
    mat3 = np.tile(np.stack(mats), (1, 1, 3))
    msk = np.stack(msks)
    return dict(
        cum3=jnp.asarray(cum3, BF16), expp=jnp.asarray(expp, BF16),
        bd=jnp.asarray(bd, BF16),
        bd2=jnp.asarray(bd2, BF16), itile=jnp.asarray(itile, F32), caus=jnp.asarray(caus, F32),
        mat3=jnp.asarray(mat3, BF16), msk=jnp.asarray(msk, F32))


def _mod_kernel(c_ref, w_ref, b_ref, o_ref):
    a = _silu(c_ref[...]).astype(BF16)
    o_ref[...] = _dot(a, w_ref[...].astype(BF16)) + b_ref[...]


def _modulation(cvec8, w_mod, b_mod):
    D = D_MODEL
    return pl.pallas_call(
        _mod_kernel,
        grid=(DEPTH, 6),
        in_specs=[
            pl.BlockSpec((8, D), lambda l, j: (0, 0)),
            pl.BlockSpec((None, D, D), lambda l, j: (l, 0, j)),
            pl.BlockSpec((None, None, 1, D), lambda l, j: (l, j, 0, 0)),
        ],
        out_specs=pl.BlockSpec((None, None, 8, D), lambda l, j: (l, j, 0, 0)),
        out_shape=jax.ShapeDtypeStruct((DEPTH, 6, 8, D), F32),
        compiler_params=_params("parallel", "parallel"),
        name="modulation",
    )(cvec8, w_mod, b_mod.reshape(DEPTH, 6, 1, D))


def _lb_kernel(x_ref, o_ref):
    x = x_ref[...]
    e = jnp.exp(x - jnp.max(x, axis=0, keepdims=True))
    p = e / jnp.sum(e, axis=0, keepdims=True)
    acc = jnp.zeros_like(p[0:1])
    rows = []
    for l in range(DEPTH):
        acc = acc + p[l:l + 1]
        rows.append(acc - p[0:1])
    o_ref[...] = jnp.concatenate(rows, axis=0)


def _hgrn_lower_bounds(logits):
    return pl.pallas_call(
        _lb_kernel, out_shape=jax.ShapeDtypeStruct(logits.shape, F32), name="hgrn_lower_bounds",
    )(logits.astype(F32))


def _absorb_kernel(a_ref, b_ref, o_ref):
    a1, a2, a3 = _split3(a_ref[...])
    b1, b2, b3 = _split3(b_ref[...])
    o_ref[...] = (_dot_t(a1, b1) + _dot_t(a1, b2) + _dot_t(a2, b1)
                  + _dot_t(a1, b3) + _dot_t(a2, b2) + _dot_t(a3, b1))


def _absorbed_query_weights(wq_nope, wk_nope):
    return pl.pallas_call(
        _absorb_kernel,
        grid=(DEPTH, MLA_H),
        in_specs=[
            pl.BlockSpec((None, None, Q_RANK, MLA_DN), lambda l, h: (l, h, 0, 0)),
            pl.BlockSpec((None, None, KV_RANK, MLA_DN), lambda l, h: (l, h, 0, 0)),
        ],
        out_specs=pl.BlockSpec((None, None, Q_RANK, KV_RANK), lambda l, h: (l, h, 0, 0)),
        out_shape=jax.ShapeDtypeStruct((DEPTH, MLA_H, Q_RANK, KV_RANK), F32),
        compiler_params=_params("parallel", "parallel"),
        name="absorb_query_weights",
    )(wq_nope, wk_nope)


def _inproj_kernel(x_ref, mod_ref, w_ref, b_ref, o_ref):
    sh, sc = mod_ref[0, 0:1, :], mod_ref[0, 1:2, :]
    h = (x_ref[...] * (1.0 + sc) + sh).astype(BF16)
    o_ref[...] = _dot(h, w_ref[...]) + b_ref[...]


def _inproj(x, mod, w, b, *, layer, tm, tiles_per_mod):
    n = x.shape[0]
    return pl.pallas_call(
        _inproj_kernel,
        grid=(n // tm,),
        in_specs=[
            pl.BlockSpec((tm, D_MODEL), lambda i: (i, 0)),
            pl.BlockSpec((1, 6, D_MODEL), lambda i: (i // tiles_per_mod, 0, 0)),
            pl.BlockSpec((None, D_MODEL, PROJ_W), lambda i: (layer, 0, 0)),
            pl.BlockSpec((None, 1, PROJ_W), lambda i: (layer, 0, 0)),
        ],
        out_specs=pl.BlockSpec((tm, PROJ_W), lambda i: (i, 0)),
        out_shape=jax.ShapeDtypeStruct((n, PROJ_W), F32),
        compiler_params=_params("parallel"),
        name="in_projection",
    )(x, mod, w, b)


def _attn_prep_kernel(cq_ref, kv_ref, qtab_ref, ktab_ref, wq_ref, qg_ref, kg_ref, q_ref, k_ref, ckv_ref):
    cq = cq_ref[...]
    cqn = cq * lax.rsqrt(jnp.mean(cq * cq, axis=-1, keepdims=True) + EPS) * qg_ref[...]
    qf = _dot(cqn.astype(BF16), wq_ref[...])
    qtab = qtab_ref[...]
    for h in range(MLA_H):
        q_ref[:, h * Q_SLOT:(h + 1) * Q_SLOT] = (qf[:, h * Q_SLOT:(h + 1) * Q_SLOT] * qtab).astype(BF16)
    ckv = kv_ref[:, :KV_RANK]
    ckvn = ckv * lax.rsqrt(jnp.mean(ckv * ckv, axis=-1, keepdims=True) + EPS) * kg_ref[...]
    ckv_ref[...] = ckvn
    k_ref[:, :KV_RANK] = ckvn.astype(BF16)
    t = kv_ref[:, KV_RANK:] * ktab_ref[...]
    kp = t + pltpu.roll(t, MLA_DR, 1) + pltpu.roll(t, LANE - MLA_DR, 1)
    lane = lax.broadcasted_iota(jnp.int32, kp.shape, 1)
    one = jnp.where(lane == ONE_LANE - KV_RANK, 1.0, 0.0)
    k_ref[:, KV_RANK:] = jnp.where(lane < 2 * MLA_DR, kp, one).astype(BF16)


def _attn_prep(proj, qtab, ktab, wq, qg, kg, *, layer, tm, tab_tiles):
    n = proj.shape[0]
    return pl.pallas_call(
        _attn_prep_kernel,
        grid=(n // tm,),
        in_specs=[
            pl.BlockSpec((tm, Q_RANK), lambda i: (i, BLK_CQ)),
            pl.BlockSpec((tm, 2 * LANE), lambda i: (i, BLK_KV)),
            pl.BlockSpec((tm, Q_SLOT), lambda i: (i % tab_tiles, 0)),
            pl.BlockSpec((tm, LANE), lambda i: (i % tab_tiles, 0)),
            pl.BlockSpec((None, Q_RANK, MLA_H * Q_SLOT), lambda i: (layer, 0, 0)),
            pl.BlockSpec((1, Q_RANK), lambda i: (0, 0)),
            pl.BlockSpec((1, KV_RANK), lambda i: (0, 0)),
        ],
        out_specs=[
            pl.BlockSpec((tm, MLA_H * Q_SLOT), lambda i: (i, 0)),
            pl.BlockSpec((tm, Q_SLOT), lambda i: (i, 0)),
            pl.BlockSpec((tm, KV_RANK), lambda i: (i, 0)),
        ],
        out_shape=[
            jax.ShapeDtypeStruct((n, MLA_H * Q_SLOT), BF16),
            jax.ShapeDtypeStruct((n, Q_SLOT), BF16),
            jax.ShapeDtypeStruct((n, KV_RANK), F32),
        ],
        compiler_params=_params("parallel"),
        name="attention_prep",
    )(proj, proj, qtab, ktab, wq, qg, kg)


def _attn_kernel(*refs, has_cache, heads, tk, lockstep):
    if has_cache:
        q_ref, k_ref, kc_ref, o_ref = refs
    else:
        q_ref, k_ref, o_ref = refs
    n_new = k_ref.shape[0]
    blocks = [(k_ref, j * tk, min(tk, n_new - j * tk)) for j in range(pl.cdiv(n_new, tk))]
    if has_cache:
        n_old = kc_ref.shape[0]
        blocks += [(kc_ref, j * tk, min(tk, n_old - j * tk)) for j in range(pl.cdiv(n_old, tk))]
    tq = q_ref.shape[0]
    def head(h):
        q = q_ref[:, h * Q_SLOT:(h + 1) * Q_SLOT]
        m = jnp.full((tq, 1), NEG_BIG, F32)
        acc = jnp.zeros((tq, Q_SLOT), F32)
        for ref, start, size in blocks:
            kj = ref[start:start + size, :]
            s = _dot_t(q, kj)
            yield
            m_new = jnp.maximum(m, jnp.max(s, axis=-1, keepdims=True))
            acc = acc * jnp.exp2(m - m_new) + _dot(jnp.exp2(s - m_new).astype(BF16), kj)
            m = m_new
        o_ref[:, h * KV_RANK:(h + 1) * KV_RANK] = (acc[:, :KV_RANK] / acc[:, ONE_LANE:ONE_LANE + 1]).astype(BF16)

    for h0 in range(0, heads, lockstep):
        _round_robin([head(h) for h in range(h0, h0 + lockstep)])


def _attention(q, k, kcache, *, n_seq, seq_len, tq, heads, lockstep, tk=512, layer=0):
    n = q.shape[0]
    qt = seq_len // tq
    in_specs = [
        pl.BlockSpec((tq, heads * Q_SLOT), lambda b, i, h: (b * qt + i, h)),
        pl.BlockSpec((seq_len, Q_SLOT), lambda b, i, h: (b, 0)),
    ]
    args = [q, k]
    if kcache is not None:
        in_specs.append(pl.BlockSpec((None, None, kcache.shape[2], Q_SLOT), lambda b, i, h: (b, layer, 0, 0)))
        args.append(kcache)
    return pl.pallas_call(
        functools.partial(_attn_kernel, has_cache=kcache is not None, heads=heads, tk=tk, lockstep=lockstep),
        grid=(n_seq, qt, MLA_H // heads),
        in_specs=in_specs,
        out_specs=pl.BlockSpec((tq, heads * KV_RANK), lambda b, i, h: (b * qt + i, h)),
        out_shape=jax.ShapeDtypeStruct((n, MLA_H * KV_RANK), BF16),
        compiler_params=_params("parallel", "parallel", "parallel"),
        name="attention",
    )(*args)


OUTPROJ_PARTS = 4


def _outproj_kernel(x_ref, ol_ref, mg_ref, mod_ref, wuv_ref, wo_ref, g_ref, b_ref, o_ref):
    g1 = mod_ref[0, 2:3, :]
    rows_per_part = x_ref.shape[0] // OUTPROJ_PARTS

    def part(r):
        rows = slice(r * rows_per_part, (r + 1) * rows_per_part)
        a = _dot(ol_ref[rows, :], wuv_ref[...]).astype(BF16)
        yield
        mix = _dot(a, wo_ref[:MLA_H * MLA_DV, :]) + _dot(mg_ref[rows, :], wo_ref[MLA_H * MLA_DV:, :])
        yield
        y = ALPHA * x_ref[rows, :] + g1 * mix
        o_ref[rows, :] = _layernorm(y, g_ref[...], b_ref[...])

    _round_robin([part(r) for r in range(OUTPROJ_PARTS)], skew=1)


def _outproj(x, o_lat, mg, mod, wuv, wo, g, b, *, layer, tm, tiles_per_mod):
    n = x.shape[0]
    D = D_MODEL
    return pl.pallas_call(
        _outproj_kernel,
        grid=(n // tm,),
        in_specs=[
            pl.BlockSpec((tm, D), lambda i: (i, 0)),
            pl.BlockSpec((tm, MLA_H * KV_RANK), lambda i: (i, 0)),
            pl.BlockSpec((tm, 2 * REC_W), lambda i: (i, 0)),
            pl.BlockSpec((1, 6, D), lambda i: (i // tiles_per_mod, 0, 0)),
            pl.BlockSpec((None, MLA_H * KV_RANK, MLA_H * MLA_DV), lambda i: (layer, 0, 0)),
            pl.BlockSpec((None, D, D), lambda i: (layer, 0, 0)),
            pl.BlockSpec((1, D), lambda i: (0, 0)),
            pl.BlockSpec((1, D), lambda i: (0, 0)),
        ],
        out_specs=pl.BlockSpec((tm, D), lambda i: (i, 0)),
        out_shape=jax.ShapeDtypeStruct((n, D), F32),
        compiler_params=_params("parallel"),
        name="out_projection",
    )(x, o_lat, mg, mod, wuv, wo, g, b)


FF_SUB = 256
FFN_PARTS = 2
FFN_SKEW = 3


def _ffn_kernel(x_ref, mod_ref, wi_ref, wo_ref, g_ref, b_ref, o_ref):
    sh, sc, g2 = mod_ref[0, 3:4, :], mod_ref[0, 4:5, :], mod_ref[0, 5:6, :]
    rows_per_part = x_ref.shape[0] // FFN_PARTS

    def part(r):
        rows = slice(r * rows_per_part, (r + 1) * rows_per_part)
        x = x_ref[rows, :]
        h = (x * (1.0 + sc) + sh).astype(BF16)
        acc = None
        for c0 in range(0, FF, FF_SUB):
            yield
            act = (_silu(_dot(h, wi_ref[:, c0:c0 + FF_SUB])) * _dot(h, wi_ref[:, FF + c0:FF + c0 + FF_SUB])).astype(BF16)
            upd = _dot(act, wo_ref[c0:c0 + FF_SUB, :])
            acc = upd if acc is None else acc + upd
        yield
        o_ref[rows, :] = _layernorm(ALPHA * x + g2 * acc, g_ref[...], b_ref[...])

    _round_robin([part(r) for r in range(FFN_PARTS)], skew=FFN_SKEW)


def _ffn(x, mod, w_in, w_out, g, b, *, layer, tm, tiles_per_mod):
    n = x.shape[0]
    D = D_MODEL
    return pl.pallas_call(
        _ffn_kernel,
        grid=(n // tm,),
        in_specs=[
            pl.BlockSpec((tm, D), lambda i: (i, 0)),
            pl.BlockSpec((1, 6, D), lambda i: (i // tiles_per_mod, 0, 0)),
            pl.BlockSpec((None, D, 2 * FF), lambda i: (layer, 0, 0), pipeline_mode=pl.Buffered(1)),
            pl.BlockSpec((None, FF, D), lambda i: (layer, 0, 0), pipeline_mode=pl.Buffered(1)),
            pl.BlockSpec((1, D), lambda i: (0, 0)),
            pl.BlockSpec((1, D), lambda i: (0, 0)),
        ],
        out_specs=pl.BlockSpec((tm, D), lambda i: (i, 0)),
        out_shape=jax.ShapeDtypeStruct((n, D), F32),
        compiler_params=_params("parallel"),
        name="ffn",
    )(x, mod, w_in, w_out, g, b)


def _gate_terms(gti_f, gtf_f, gti_b, gtf_b, cum3_ref):
    lane = lax.broadcasted_iota(jnp.int32, (CHUNK, LANE), 1)
    is_f = lane < N_HEADS
    ig = jnp.where(is_f, gti_f, gti_b)
    lf = _log_sigmoid(jnp.where(is_f, gtf_f, gtf_b))
    b = jnp.where(is_f, _sel_rows(cum3_ref[0], lf), _sel_rows(cum3_ref[1], lf))
    return is_f, ig, lf, b


def _hgrn_gates(fr, lb):
    f = lb + (1.0 - lb) * jax.nn.sigmoid(fr)
    kk = (1.0 - lb) * jax.nn.sigmoid(-fr)
    return kk, jnp.log(jnp.maximum(f, TINY))


def _scan_kernel(*refs, nc, group, zero_init, emit_final):
    it = iter(refs)
    mk = (next(it), next(it))
    mv = (next(it), next(it))
    gf = (next(it), next(it))
    gi = (next(it), next(it))
    gti = (next(it), next(it))
    gtf = (next(it), next(it))
    lb_ref, cum3_ref, expp_ref, bd_ref = next(it), next(it), next(it), next(it)
    if not zero_init:
        c0_ref, n0_ref, m0_ref, s0_ref = next(it), next(it), next(it), next(it)
    cs = (next(it), next(it))
    ns = (next(it), next(it))
    ms = (next(it), next(it))
    ss = (next(it), next(it))
    if emit_final:
        cfin_ref, nfin_ref, mfin_ref, sfin_ref = next(it), next(it), next(it), next(it)
    c_scr, n_scr, m_scr, s_scr = next(it), next(it), next(it), next(it)

    pos = pl.program_id(1)

    @pl.when(pos == 0)
    def _():
        if zero_init:
            c_scr[...] = jnp.zeros_like(c_scr)
            n_scr[...] = jnp.zeros_like(n_scr)
            m_scr[...] = jnp.zeros_like(m_scr)
            s_scr[...] = jnp.zeros_like(s_scr)
        else:
            c_scr[...] = c0_ref[...]
            n_scr[...] = n0_ref[...]
            m_scr[...] = m0_ref[...]
            s_scr[...] = s0_ref[...]

    bd = bd_ref[...]
    lb = lb_ref[...]
    def seq_step(q):
        _, ig, lf, b = _gate_terms(gti[0][q], gtf[0][q], gti[1][q], gtf[1][q], cum3_ref)
        hg = [_hgrn_gates(gf[d][q], lb) for d in range(2)]
        bcs = [_sel_rows(cum3_ref[d], hg[d][1]) for d in range(2)]
        yield
        r = ig - b
        rmax = jnp.max(r, axis=0, keepdims=True)
        g = jnp.sum(lf, axis=0, keepdims=True)
        m = m_scr[q]
        mm = jnp.maximum(m, rmax)
        sc = jnp.exp(m - mm)
        w = jnp.exp(r - mm)
        ms[0][q] = m
        ms[1][q] = m
        m_scr[q] = g + mm
        wsc = jnp.concatenate([w, jnp.broadcast_to(sc, (16, LANE))], axis=0)
        wxs = [_expand_heads(wsc, expp_ref[d]) for d in range(2)]
        s_olds, s_upds, gls = [], [], []
        for d in range(2):
            kk, lg = hg[d]
            gl = jnp.sum(lg, axis=0, keepdims=True)
            kd = kk * jnp.exp(gl - bcs[d])
            s_old = s_scr[q, d]
            ss[d][q] = s_old.astype(BF16) * bd
            s_olds.append(s_old)
            gls.append(gl)
            s_upds.append(_dot(gi[d][q].T.astype(BF16), kd.astype(BF16)))
        yield
        news = []
        for d in range(2):
            c_old = c_scr[q, d]
            n_old = n_scr[q, d]
            cs[d][q] = c_old.astype(BF16) * bd
            ns[d][q] = n_old
            scx = wxs[d][CHUNK:CHUNK + 1]
            kw = mk[d][q] * (HEAD_D ** -0.5) * wxs[d][:CHUNK]
            c_upd = _dot(kw.T.astype(BF16), mv[d][q].astype(BF16))
            n_new = n_old * scx + jnp.sum(kw, axis=0, keepdims=True)
            news.append((c_old, scx, c_upd, n_new))
        yield
        for d in range(2):
            c_old, scx, c_upd, n_new = news[d]
            c_new = c_old * scx + c_upd
            s_new = s_olds[d] * jnp.exp(gls[d]) + s_upds[d]
            c_scr[q, d] = c_new
            n_scr[q, d] = n_new
            s_scr[q, d] = s_new
            if emit_final:
                @pl.when(pos == nc - 1)
                def _():
                    nfin_ref[q, d] = n_new
                    for h in range(N_HEADS):
                        hs = slice(h * HEAD_D, (h + 1) * HEAD_D)
                        cfin_ref[q, d, h] = c_new[hs, hs]
                        sfin_ref[q, d, h] = s_new[hs, hs].T
        if emit_final:
            @pl.when(pos == nc - 1)
            def _():
                mfin_ref[q] = g + mm

    _round_robin([seq_step(q) for q in range(group)])


def _state_scan(proj, lb, consts, init, *, n_seq, nc, group, emit_final, layer=0):
    zero_init = init is None
    p4 = proj.reshape(n_seq, nc, CHUNK, PROJ_W)

    def fwd(p):
        return p

    def bwd(p):
        return nc - 1 - p

    def blk(col, w, idx):
        return pl.BlockSpec((group, None, CHUNK, w), lambda s, p: (s, idx(p), 0, col))

    in_specs, args = [], []
    for col, w in ((BLK_MK, REC_W), (BLK_MV, REC_W)):
        for idx in (fwd, bwd):
            in_specs.append(blk(col, w, idx))
            args.append(p4)
    in_specs += [blk(BLK_GFF, REC_W, fwd), blk(BLK_GFB, REC_W, bwd)]
    args += [p4, p4]
    for col, w in ((BLK_GI, REC_W), (BLK_GATE_I, LANE), (BLK_GATE_F, LANE)):
        for idx in (fwd, bwd):
            in_specs.append(blk(col, w, idx))
            args.append(p4)
    in_specs += [
        pl.BlockSpec((1, REC_W), lambda s, p: (0, 0)),
        pl.BlockSpec((2, CHUNK, 3 * CHUNK), lambda s, p: (0, 0, 0)),
        pl.BlockSpec((2, LANE, REC_W), lambda s, p: (0, 0, 0)),
        pl.BlockSpec((REC_W, REC_W), lambda s, p: (0, 0)),
    ]
    args += [lb, consts["cum3"], consts["expp"], consts["bd"]]
    state_shapes = ((2, REC_W, REC_W), (2, 1, REC_W), (1, LANE), (2, REC_W, REC_W))
    if not zero_init:
        for shape in state_shapes:
            in_specs.append(pl.BlockSpec((group, None) + shape, lambda s, p, k=len(shape): (s, layer) + (0,) * k))
        args += list(init)

    out_specs, out_shape = [], []

    def add_out(shape, dtype):
        for idx in (fwd, bwd):
            out_specs.append(pl.BlockSpec((group, None) + shape, lambda s, p, idx=idx: (s, idx(p)) + (0,) * len(shape)))
            out_shape.append(jax.ShapeDtypeStruct((n_seq, nc) + shape, dtype))

    add_out((REC_W, REC_W), BF16)
    add_out((1, REC_W), F32)
    add_out((1, LANE), F32)
    add_out((REC_W, REC_W), BF16)
    if emit_final:
        per_head = (2, N_HEADS, HEAD_D, HEAD_D)
        for shape in (per_head, state_shapes[1], state_shapes[2], per_head):
            out_specs.append(pl.BlockSpec((group,) + shape, lambda s, p, k=len(shape): (s,) + (0,) * k))
            out_shape.append(jax.ShapeDtypeStruct((n_seq,) + shape, F32))

    outs = pl.pallas_call(
        functools.partial(_scan_kernel, nc=nc, group=group, zero_init=zero_init, emit_final=emit_final),
        grid=(n_seq // group, nc),
        in_specs=in_specs,
        out_specs=out_specs,
        out_shape=out_shape,
        scratch_shapes=[pltpu.VMEM((group,) + shape, F32) for shape in state_shapes],
        compiler_params=_params("parallel", "arbitrary"),
        name="state_scan",
    )(*args)
    return list(outs)


def _round_robin(gens, skew=0):
    results = [None] * len(gens)
    done = [False] * len(gens)
    rnd = 0
    while not all(done):
        for i, g in enumerate(gens):
            if done[i] or rnd < skew * i:
                continue
            try:
                next(g)
            except StopIteration as e:
                results[i] = e.value
                done[i] = True
        rnd += 1
    return results


def _mlstm_chunk(mq, mk, mv, mo, gti, gtf, csf, csb, nsf, nsb, msf, msb,
                 mnorm, cum3_ref, expp_ref, bd, bd2, itile, caus_ref):
    inv_d = 1.0 / HEAD_D
    is_f, ig, lf, b = _gate_terms(gti, gtf, gti, gtf, cum3_ref)
    yield
    r = ig - b
    row = lax.broadcasted_iota(jnp.int32, (CHUNK, LANE), 0)
    cm_f = r
    cm_b = r
    sh = 1
    while sh < CHUNK:
        cm_f = jnp.maximum(cm_f, jnp.where(row >= sh, pltpu.roll(cm_f, sh, 0), NEG_BIG))
        cm_b = jnp.maximum(cm_b, jnp.where(row < CHUNK - sh, pltpu.roll(cm_b, CHUNK - sh, 0), NEG_BIG))
        sh *= 2
    m_row = jnp.where(is_f[0:1], msf, msb)
    big_m = jnp.maximum(m_row, jnp.where(is_f, cm_f, cm_b))
    ws = jnp.exp(m_row - big_m)
    stacked = jnp.concatenate([big_m, ws, b + big_m, r], axis=0)

    qb = mq.astype(BF16)
    k_bd = _block_diag((mk * (HEAD_D ** -0.5)).astype(BF16), bd)
    v_bd = _block_diag(mv.astype(BF16), bd)
    s = _dot_t(qb, k_bd)
    ones3 = jnp.ones((CHUNK, 3 * CHUNK), BF16)
    ex = [_expand_heads(stacked, expp_ref[d]) for d in range(2)]
    yield
    r_bcast = [_sel_rows(ones3, ex[d][3 * CHUNK:] * itile) for d in range(2)]
    yield
    p = []
    for d in range(2):
        arg = jnp.where(caus_ref[d] > 0.0, r_bcast[d] - ex[d][:CHUNK], NEG_BIG)
        p.append(s * jnp.exp(arg))
    pv = _dot(jnp.concatenate(p, axis=0).astype(BF16), v_bd)
    qc = [_dot(qb, c_st) for c_st in (csf, csb)]
    all_sums = _seg_sum(jnp.concatenate([mq * nsf, p[0], mq * nsb, p[1]], axis=0), bd2)
    sums = [(all_sums[2 * d * CHUNK:(2 * d + 1) * CHUNK], all_sums[(2 * d + 1) * CHUNK:(2 * d + 2) * CHUNK])
            for d in range(2)]
    yield
    hsum = None
    for d in range(2):
        wsx = ex[d][CHUNK:2 * CHUNK]
        num = wsx * qc[d] + pv[d * CHUNK:(d + 1) * CHUNK]
        den = wsx * sums[d][0] + sums[d][1]
        hd = num / jnp.maximum(jnp.abs(den), jnp.exp(-ex[d][2 * CHUNK:3 * CHUNK]))
        hsum = hd if hsum is None else hsum + hd
    mu = _seg_sum(hsum, bd2) * inv_d
    yield
    hc = hsum - mu
    var = _seg_sum(hc * hc, bd2) * inv_d
    yield
    return hc * lax.rsqrt(var + EPS) * mnorm * jax.nn.sigmoid(mo)


def _hgrn_chunk(gq, gff, gfb, gi, gg, ssf, ssb, lb, hnorm, bd, bd2, mat3_ref, msk_ref):
    inv_d = 1.0 / HEAD_D
    qh = _silu(gq)
    gv_bd = _block_diag(gi.astype(BF16), bd)
    gates = [_hgrn_gates(fr, lb) for fr in (gff, gfb)]
    args = [_sel_rows(mat3_ref[d], gates[d][1]) for d in range(2)]
    yield
    a_tot = None
    inter = None
    for d, s_st in enumerate((ssf, ssb)):
        kk = gates[d][0]
        e = jnp.exp(args[d])

        def eb(i):
            return e[i * CHUNK:(i + 1) * CHUNK]

        t_in = _dot_t((qh * eb(0)).astype(BF16), s_st)
        inter = t_in if inter is None else inter + t_in
        stages = (
            (kk * eb(1), [qh * eb(2), qh * eb(3), qh * eb(4)]),
            (kk * eb(5), [qh * eb(6), qh * eb(7), qh * eb(8)]),
            (kk, [qh * eb(9), qh * eb(10), qh * eb(11), qh]),
        )
        outs = [_dot_t(jnp.concatenate(q_list, axis=0).astype(BF16), _block_diag(k_st.astype(BF16), bd))
                for k_st, q_list in stages]
        yield
        mi = 0
        for out, (_, q_list) in zip(outs, stages):
            for j in range(len(q_list)):
                term = msk_ref[d, mi] * out[j * CHUNK:(j + 1) * CHUNK]
                a_tot = term if a_tot is None else a_tot + term
                mi += 1
    o = _dot(a_tot.astype(BF16), gv_bd) + inter
    yield
    ms = _seg_sum(o * o, bd2) * inv_d
    yield
    return o * lax.rsqrt(ms + EPS) * hnorm * _silu(gg)


def _mixer_kernel(*refs, chunks):
    mq, mk, mv, mo, gq, gff, gfb, gi, gg, gti, gtf = refs[:11]
    csf, csb, nsf, nsb, msf, msb, ssf, ssb = refs[11:19]
    lb_ref, mnorm_ref, hnorm_ref, cum3_ref, expp_ref, bd_ref, bd2_ref, itile_ref, caus_ref, mat3_ref, msk_ref = refs[19:30]
    o_ref = refs[30]
    bd, bd2, itile = bd_ref[...], bd2_ref[...], itile_ref[...]
    lb, mnorm, hnorm = lb_ref[...], mnorm_ref[...], hnorm_ref[...]
    gens = []
    for c in range(chunks):
        rows = slice(c * CHUNK, (c + 1) * CHUNK)
        gens.append(_mlstm_chunk(*[r[rows, :] for r in (mq, mk, mv, mo, gti, gtf)],
                                 *[r[c] for r in (csf, csb, nsf, nsb, msf, msb)],
                                 mnorm, cum3_ref, expp_ref, bd, bd2, itile, caus_ref))
        gens.append(_hgrn_chunk(*[r[rows, :] for r in (gq, gff, gfb, gi, gg)], ssf[c], ssb[c],
                                lb, hnorm, bd, bd2, mat3_ref, msk_ref))
    outs = _round_robin(gens)
    for c in range(chunks):
        rows = slice(c * CHUNK, (c + 1) * CHUNK)
        o_ref[rows, :REC_W] = outs[2 * c].astype(BF16)
        o_ref[rows, REC_W:] = outs[2 * c + 1].astype(BF16)


def _mixer_outputs(proj, states, lb, mnorm, hnorm, consts, *, chunks):
    n = proj.shape[0]
    rows = chunks * CHUNK

    def blk(col, w):
        return pl.BlockSpec((rows, w), lambda c: (c, col))

    in_specs = [blk(c, REC_W) for c in (BLK_MQ, BLK_MK, BLK_MV, BLK_MO, BLK_GQ, BLK_GFF, BLK_GFB, BLK_GI, BLK_GG)]
    in_specs += [blk(BLK_GATE_I, LANE), blk(BLK_GATE_F, LANE)]
    args = [proj] * 11
    for arr in states:
        shape = arr.shape[2:]
        per_seq = arr.shape[1] // chunks
        in_specs.append(pl.BlockSpec((None, chunks) + shape,
                                     lambda c, k=len(shape), per_seq=per_seq: (c // per_seq, c % per_seq) + (0,) * k))
        args.append(arr)

    def whole(arr):
        return pl.BlockSpec(arr.shape, lambda c, k=arr.ndim: (0,) * k)

    for arr in (lb, mnorm, hnorm, consts["cum3"], consts["expp"], consts["bd"], consts["bd2"], consts["itile"],
                consts["caus"], consts["mat3"], consts["msk"]):
        in_specs.append(whole(arr))
        args.append(arr)
    return pl.pallas_call(
        functools.partial(_mixer_kernel, chunks=chunks),
        grid=(n // rows,),
        in_specs=in_specs,
        out_specs=pl.BlockSpec((rows, 2 * REC_W), lambda c: (c, 0)),
        out_shape=jax.ShapeDtypeStruct((n, 2 * REC_W), BF16),
        compiler_params=_params("parallel"),
        name="mixer_outputs",
    )(*args)


def _rot_perm():
    q = MLA_DR // 4
    idx = np.concatenate([np.arange(q, 2 * q), np.arange(0, q), np.arange(3 * q, 4 * q), np.arange(2 * q, 3 * q)])
    sign = np.concatenate([-np.ones(q), np.ones(q), -np.ones(q), np.ones(q)]).astype(np.float32)
    return idx, sign


def _pad_cols(a, width):
    return jnp.pad(a, [(0, 0)] * (a.ndim - 1) + [(0, width - a.shape[-1])])


def _layout_in_proj(w_in, b_in):
    off = np.concatenate([[0], np.cumsum(IN_SIZES)])
    idx, sign = _rot_perm()

    def lay(a):
        def cols(i, j=None):
            return a[..., int(off[i]):int(off[(i if j is None else j) + 1])]

        kpe = cols(2)
        return jnp.concatenate([
            cols(0), cols(1),
            _pad_cols(jnp.concatenate([kpe, kpe[..., idx] * sign], axis=-1), LANE),
            cols(3, 6), cols(11, 15),
            _pad_cols(jnp.concatenate([cols(7), cols(8)], axis=-1), LANE),
            _pad_cols(jnp.concatenate([cols(9), cols(10)], axis=-1), LANE),
        ], axis=-1)

    return lay(w_in).astype(BF16), lay(b_in)[:, None, :]


def _layout_query_weights(w_uq, w_abs):
    idx, sign = _rot_perm()
    w = w_uq.reshape(DEPTH, Q_RANK, MLA_H, MLA_DN + MLA_DR)
    pe = w[..., MLA_DN:]
    slot = jnp.concatenate([jnp.transpose(w_abs, (0, 2, 1, 3)), pe, pe[..., idx] * sign], axis=-1)
    return _pad_cols(slot, Q_SLOT).reshape(DEPTH, Q_RANK, MLA_H * Q_SLOT).astype(BF16)


def _layout_value_weights(w_ukv):
    w = w_ukv.reshape(DEPTH, KV_RANK, MLA_H, MLA_DN + MLA_DV)[..., MLA_DN:]
    eye = jnp.eye(MLA_H, dtype=w.dtype)
    bdw = jnp.einsum("lrhe,hg->lhrge", w, eye)
    return bdw.reshape(DEPTH, MLA_H * KV_RANK, MLA_H * MLA_DV).astype(BF16)


def _rope_tables(n):
    n_rows = n // GRID_W
    row = jnp.repeat(jnp.arange(n_rows, dtype=F32), GRID_W)
    col = jnp.tile(jnp.arange(GRID_W, dtype=F32), n_rows)
    half = MLA_DR // 2
    freqs = ROPE_BASE ** (-jnp.arange(half // 2, dtype=F32) * (2.0 / half))
    ar = row[:, None] * freqs
    ac = col[:, None] * freqs
    ang = jnp.concatenate([ar, ar, ac, ac], -1)
    return jnp.cos(ang), jnp.sin(ang)


def _block_diag_states(x):
    eye = jnp.eye(N_HEADS, dtype=x.dtype)
    y = jnp.einsum("...hde,hg->...hdge", x, eye)
    return y.reshape(x.shape[:-3] + (REC_W, REC_W))


def kernel(x_prompt, x_sample, cache_mla_ckv, cache_mla_kpe, state_mlstm_C, state_mlstm_n, state_mlstm_m, state_hgrn_S, c, c_ctx, w_mod, b_mod, w_in, b_in, mla_q_norm, w_uq, mla_kv_norm, w_ukv, mlstm_norm, hgrn_lb_logits, hgrn_norm, w_out, ln1_g, ln1_b, w_ffn_in, w_ffn_out, ln2_g, ln2_b):
    B, T, D = x_prompt.shape
    Bs, Ts, _ = x_sample.shape
    past = cache_mla_ckv.shape[2]
    consts = _recurrent_constants()

    cvec = jnp.concatenate([c_ctx[None, :], c, jnp.zeros((8 - 1 - Bs, D), F32)], axis=0)
    mod = jnp.transpose(_modulation(cvec, w_mod, b_mod), (0, 2, 1, 3))
    lbs = _hgrn_lower_bounds(hgrn_lb_logits)

    w_in_p, b_in_p = _layout_in_proj(w_in, b_in)
    wq4 = w_uq.reshape(DEPTH, Q_RANK, MLA_H, MLA_DN + MLA_DR)
    wkv4 = w_ukv.reshape(DEPTH, KV_RANK, MLA_H, MLA_DN + MLA_DV)
    w_abs = _absorbed_query_weights(jnp.transpose(wq4[..., :MLA_DN], (0, 2, 1, 3)),
                                    jnp.transpose(wkv4[..., :MLA_DN], (0, 2, 1, 3)))
    wq_p = _layout_query_weights(w_uq, w_abs)
    wuv_p = _layout_value_weights(w_ukv)
    w_out_b = w_out.astype(BF16)
    w_ffn_in_b = w_ffn_in.astype(BF16)
    w_ffn_out_b = w_ffn_out.astype(BF16)

    scale = (MLA_DN + MLA_DR) ** -0.5 * float(np.log2(np.e))
    cos, sin = _rope_tables(Ts)
    qtab_s = scale * jnp.concatenate([jnp.ones((Ts, KV_RANK), F32), cos, sin, jnp.zeros((Ts, Q_SLOT - KV_RANK - 2 * MLA_DR), F32)], axis=1)
    ktab_s = jnp.concatenate([cos, sin, jnp.zeros((Ts, LANE - 2 * MLA_DR), F32)], axis=1)
    TM_C = 256
    lane_q = np.arange(Q_SLOT)
    qtab_c = jnp.asarray(np.tile(scale * (lane_q < KV_RANK + MLA_DR), (TM_C, 1)), F32)
    ktab_c = jnp.asarray(np.tile(np.arange(LANE) < MLA_DR, (TM_C, 1)), F32)

    kcache = jnp.concatenate([cache_mla_ckv, cache_mla_kpe, cache_mla_kpe,
                              jnp.ones(cache_mla_kpe.shape[:-1] + (1,), F32),
                              jnp.zeros(cache_mla_kpe.shape[:-1] + (Q_SLOT - ONE_LANE - 1,), F32)], axis=-1).astype(BF16)
    c0 = _block_diag_states(state_mlstm_C)
    n0 = state_mlstm_n.reshape(Bs, DEPTH, 2, 1, REC_W)
    m0 = _pad_cols(state_mlstm_m.reshape(Bs, DEPTH, 1, 2 * N_HEADS), LANE)
    s0 = _block_diag_states(jnp.swapaxes(state_hgrn_S, -1, -2))

    xp = x_prompt.reshape(B * T, D)
    xs = x_sample.reshape(Bs * Ts, D)
    nc_c, nc_s = T // CHUNK, Ts // CHUNK
    TM = 512
    TM_F = 1024
    ckv_l, kpe_l, fin_l = [], [], []
    for l in range(DEPTH):
        lb = lbs[l][None, :]
        mnorm, hnorm = mlstm_norm[l][None, :], hgrn_norm[l][None, :]
        qg, kg = mla_q_norm[l][None, :], mla_kv_norm[l][None, :]
        g1, b1, g2, b2 = ln1_g[l][None, :], ln1_b[l][None, :], ln2_g[l][None, :], ln2_b[l][None, :]
        mod_c, mod_s = mod[l, 0:1], mod[l, 1:1 + Bs]

        proj = _inproj(xp, mod_c, w_in_p, b_in_p, layer=l, tm=TM, tiles_per_mod=B * T // TM)
        qa, ka, ckvn = _attn_prep(proj, qtab_c, ktab_c, wq_p, qg, kg, layer=l, tm=TM_C, tab_tiles=1)
        o_lat = _attention(qa, ka, None, n_seq=B, seq_len=T, tq=T, heads=MLA_H, lockstep=MLA_H)
        scan = _state_scan(proj, lb, consts, None, n_seq=B, nc=nc_c, group=4, emit_final=True)
        mg = _mixer_outputs(proj, scan[:8], lb, mnorm, hnorm, consts, chunks=4)
        x1 = _outproj(xp, o_lat, mg, mod_c, wuv_p, w_out_b, g1, b1, layer=l, tm=TM_F, tiles_per_mod=B * T // TM_F)
        xp = _ffn(x1, mod_c, w_ffn_in_b, w_ffn_out_b, g2, b2, layer=l, tm=TM_F, tiles_per_mod=B * T // TM_F)
        ckv_l.append(ckvn.reshape(B, T, KV_RANK))
        kpe_l.append(proj[:, 3 * LANE:3 * LANE + MLA_DR].reshape(B, T, MLA_DR))
        fin_l.append(scan[8:])

        proj = _inproj(xs, mod_s, w_in_p, b_in_p, layer=l, tm=TM, tiles_per_mod=Ts // TM)
        qa, ka, _ = _attn_prep(proj, qtab_s, ktab_s, wq_p, qg, kg, layer=l, tm=TM, tab_tiles=Ts // TM)
        o_lat = _attention(qa, ka, kcache, n_seq=Bs, seq_len=Ts, tq=256, heads=8, lockstep=1, layer=l)
        scan = _state_scan(proj, lb, consts, (c0, n0, m0, s0), n_seq=Bs, nc=nc_s, group=Bs, emit_final=False, layer=l)
        mg = _mixer_outputs(proj, scan[:8], lb, mnorm, hnorm, consts, chunks=4)
        x1 = _outproj(xs, o_lat, mg, mod_s, wuv_p, w_out_b, g1, b1, layer=l, tm=TM_F, tiles_per_mod=Ts // TM_F)
        xs = _ffn(x1, mod_s, w_ffn_in_b, w_ffn_out_b, g2, b2, layer=l, tm=TM_F, tiles_per_mod=Ts // TM_F)

    new_ckv = jnp.stack(ckv_l, axis=1)
    new_kpe = jnp.stack(kpe_l, axis=1)
    new_c = jnp.stack([f[0] for f in fin_l], axis=1)
    new_n = jnp.stack([f[1].reshape(B, 2, N_HEADS, HEAD_D) for f in fin_l], axis=1)
    new_m = jnp.stack([f[2][:, 0, :2 * N_HEADS].reshape(B, 2, N_HEADS) for f in fin_l], axis=1)
    new_s = jnp.stack([f[3] for f in fin_l], axis=1)
    return (xp.reshape(B, T, D), xs.reshape(Bs, Ts, D), new_ckv, new_kpe, new_c, new_n, new_m, new_s)
```

```python
import functools

import numpy as np
import jax
import jax.numpy as jnp
from jax import lax
from jax.experimental import pallas as pl
from jax.experimental.pallas import tpu as pltpu

F32 = jnp.float32
BF16 = jnp.bfloat16

D_MODEL = 1024
DEPTH = 4
GRID_W = 64
MLA_H = 8
MLA_DN = 64
MLA_DR = 32
MLA_DV = 64
Q_RANK = 256
KV_RANK = 128
N_HEADS = 4
HEAD_D = 64
REC_W = N_HEADS * HEAD_D
FF = 2816
CHUNK = 64
ROPE_BASE = 10000.0
ALPHA = (2 * DEPTH) ** 0.25
EPS = 1e-6
TINY = 1e-30
NEG_BIG = -1e30
IN_SIZES = (256, 128, 32, 256, 256, 256, 256, 4, 4, 4, 4, 256, 256, 256, 256, 256)
LANE = 128
Q_SLOT = 2 * LANE
ONE_LANE = KV_RANK + 2 * MLA_DR
PROJ_W = 24 * LANE
VMEM_LIMIT = 56 * 1024 * 1024

BLK_CQ, BLK_KV, BLK_MQ, BLK_MK, BLK_MV, BLK_MO, BLK_GQ, BLK_GFF, BLK_GFB, BLK_GI, BLK_GG = range(11)
BLK_GATE_I, BLK_GATE_F = 22, 23


def _params(*sem):
    return pltpu.CompilerParams(dimension_semantics=sem, vmem_limit_bytes=VMEM_LIMIT)


def _split3(x):
    x1 = x.astype(BF16)
    r1 = x - x1.astype(F32)
    x2 = r1.astype(BF16)
    x3 = (r1 - x2.astype(F32)).astype(BF16)
    return x1, x2, x3


def _dot(a, b):
    return jnp.dot(a, b, preferred_element_type=F32)


def _dot_t(a, b):
    return lax.dot_general(a, b, (((1,), (1,)), ((), ())), preferred_element_type=F32)


def _sel_rows(sel3, x):
    return _dot(sel3, jnp.concatenate(_split3(x), axis=0))


def _expand_heads(x, expp):
    x1 = x.astype(BF16).astype(F32)
    r1 = x - x1
    x2 = r1.astype(BF16).astype(F32)
    x3 = (r1 - x2).astype(BF16).astype(F32)
    lane = lax.broadcasted_iota(jnp.int32, x.shape, 1)
    n = 2 * N_HEADS
    packed = jnp.where(lane < n, x1, jnp.where(lane < 2 * n, pltpu.roll(x2, n, 1), pltpu.roll(x3, 2 * n, 1)))
    return _dot(packed.astype(BF16), expp)


def _seg_sum(x, bd2):
    x1 = x.astype(BF16)
    x2 = (x - x1.astype(F32)).astype(BF16)
    return _dot(jnp.concatenate([x1, x2], axis=1), bd2)


def _block_diag(x_bf16, bd):
    return jnp.concatenate([x_bf16] * N_HEADS, axis=0) * bd


def _log_sigmoid(x):
    return jnp.minimum(x, 0.0) - jnp.log1p(jnp.exp(-jnp.abs(x)))


def _silu(x):
    return x * jax.nn.sigmoid(x)


def _layernorm(y, g, b):
    mu = jnp.mean(y, axis=-1, keepdims=True)
    yc = y - mu
    var = jnp.mean(yc * yc, axis=-1, keepdims=True)
    return yc * lax.rsqrt(var + EPS) * g + b


def _recurrent_constants():
    L = CHUNK
    t = np.arange(L)
    tri = (t[None, :] <= t[:, None]).astype(np.float32)
    cum = np.stack([tri, tri.T])
    cum3 = np.tile(cum, (1, 1, 3))

    lane = np.arange(REC_W)
    expp = np.zeros((2, LANE, REC_W), np.float32)
    for d in range(2):
        for j in range(3):
            expp[d, 2 * N_HEADS * j + d * N_HEADS + lane // HEAD_D, lane] = 1.0

    bd = (lane[:, None] // HEAD_D == lane[None, :] // HEAD_D).astype(np.float32)
    bd2 = np.tile(bd, (2, 1))
    itile = (t[:, None] == (lane % HEAD_D)[None, :]).astype(np.float32)
    s_of_lane = lane % HEAD_D
    caus = np.stack([(s_of_lane[None, :] <= t[:, None]), (s_of_lane[None, :] >= t[:, None])]).astype(np.float32)

    def stage(G, g):
        ka = np.zeros((L, L), np.float32)
        qa = np.zeros((3, L, L), np.float32)
        msk = np.zeros((3, L, L), np.float32)
        for s in range(L):
            e = (s // g) * g + g - 1
            ka[s, s + 1:e + 1] = 1.0
        for tt in range(L):
            p = (tt % G) // g
            for j in range(3):
                if p > j:
                    E = (tt // G) * G + (j + 1) * g - 1
                    qa[j, tt, E + 1:tt + 1] = 1.0
                    for s in range(L):
                        if s // G == tt // G and (s % G) // g == j:
                            msk[j, tt, s] = 1.0
        return ka, qa, msk

    mats, msks = [], []
    for d in range(2):
        perm = t if d == 0 else t[::-1]

        def mir(m):
            return m[np.ix_(perm, perm)]

        ka16, qa16, m16 = stage(64, 16)
        ka4, qa4, m4 = stage(16, 4)
        _, qa1, m1 = stage(4, 1)
        blocks = [tri, ka16, qa16[0], qa16[1], qa16[2], ka4, qa4[0], qa4[1], qa4[2], qa1[0], qa1[1], qa1[2]]
        mats.append(np.concatenate([mir(b) for b in blocks], axis=0))
        mk = [m16[0], m16[1], m16[2], m4[0], m4[1], m4[2], m1[0], m1[1], m1[2], np.eye(L, dtype=np.float32)]
        msks.append(np.stack([np.tile(mir(m), (1, N_HEADS)) for m in mk]))
    mat3 = np.tile(np.stack(mats), (1, 1, 3))
    msk = np.stack(msks)
    return dict(
        cum3=jnp.asarray(cum3, BF16), expp=jnp.asarray(expp, BF16),
        bd=jnp.asarray(bd, BF16),
        bd2=jnp.asarray(bd2, BF16), itile=jnp.asarray(itile, F32), caus=jnp.asarray(caus, F32),
        mat3=jnp.asarray(mat3, BF16), msk=jnp.asarray(msk, F32))


def _mod_kernel(c_ref, w_ref, b_ref, o_ref):
    a = _silu(c_ref[...]).astype(BF16)
    o_ref[...] = _dot(a, w_ref[...].astype(BF16)) + b_ref[...]


def _modulation(cvec8, w_mod, b_mod):
    D = D_MODEL
    return pl.pallas_call(
        _mod_kernel,
        grid=(DEPTH, 6),
        in_specs=[
            pl.BlockSpec((8, D), lambda l, j: (0, 0)),
            pl.BlockSpec((None, D, D), lambda l, j: (l, 0, j)),
            pl.BlockSpec((None, None, 1, D), lambda l, j: (l, j, 0, 0)),
        ],
        out_specs=pl.BlockSpec((None, None, 8, D), lambda l, j: (l, j, 0, 0)),
        out_shape=jax.ShapeDtypeStruct((DEPTH, 6, 8, D), F32),
        compiler_params=_params("parallel", "parallel"),
        name="modulation",
    )(cvec8, w_mod, b_mod.reshape(DEPTH, 6, 1, D))


def _lb_kernel(x_ref, o_ref):
    x = x_ref[...]
    e = jnp.exp(x - jnp.max(x, axis=0, keepdims=True))
    p = e / jnp.sum(e, axis=0, keepdims=True)
    acc = jnp.zeros_like(p[0:1])
    rows = []
    for l in range(DEPTH):
        acc = acc + p[l:l + 1]
        rows.append(acc - p[0:1])
    o_ref[...] = jnp.concatenate(rows, axis=0)


def _hgrn_lower_bounds(logits):
    return pl.pallas_call(
        _lb_kernel, out_shape=jax.ShapeDtypeStruct(logits.shape, F32), name="hgrn_lower_bounds",
    )(logits.astype(F32))


def _absorb_kernel(a_ref, b_ref, o_ref):
    a1, a2, a3 = _split3(a_ref[...])
    b1, b2, b3 = _split3(b_ref[...])
    o_ref[...] = (_dot_t(a1, b1) + _dot_t(a1, b2) + _dot_t(a2, b1)
                  + _dot_t(a1, b3) + _dot_t(a2, b2) + _dot_t(a3, b1))


def _absorbed_query_weights(wq_nope, wk_nope):
    return pl.pallas_call(
        _absorb_kernel,
        grid=(DEPTH, MLA_H),
        in_specs=[
            pl.BlockSpec((None, None, Q_RANK, MLA_DN), lambda l, h: (l, h, 0, 0)),
            pl.BlockSpec((None, None, KV_RANK, MLA_DN), lambda l, h: (l, h, 0, 0)),
        ],
        out_specs=pl.BlockSpec((None, None, Q_RANK, KV_RANK), lambda l, h: (l, h, 0, 0)),
        out_shape=jax.ShapeDtypeStruct((DEPTH, MLA_H, Q_RANK, KV_RANK), F32),
        compiler_params=_params("parallel", "parallel"),
        name="absorb_query_weights",
    )(wq_nope, wk_nope)


def _inproj_kernel(x_ref, mod_ref, w_ref, b_ref, o_ref):
    sh, sc = mod_ref[0, 0:1, :], mod_ref[0, 1:2, :]
    h = (x_ref[...] * (1.0 + sc) + sh).astype(BF16)
    o_ref[...] = _dot(h, w_ref[...]) + b_ref[...]


def _inproj(x, mod, w, b, *, layer, tm, tiles_per_mod):
    n = x.shape[0]
    return pl.pallas_call(
        _inproj_kernel,
        grid=(n // tm,),
        in_specs=[
            pl.BlockSpec((tm, D_MODEL), lambda i: (i, 0)),
            pl.BlockSpec((1, 6, D_MODEL), lambda i: (i // tiles_per_mod, 0, 0)),
            pl.BlockSpec((None, D_MODEL, PROJ_W), lambda i: (layer, 0, 0)),
            pl.BlockSpec((None, 1, PROJ_W), lambda i: (layer, 0, 0)),
        ],
        out_specs=pl.BlockSpec((tm, PROJ_W), lambda i: (i, 0)),
        out_shape=jax.ShapeDtypeStruct((n, PROJ_W), F32),
        compiler_params=_params("parallel"),
        name="in_projection",
    )(x, mod, w, b)


def _attn_prep_kernel(cq_ref, kv_ref, qtab_ref, ktab_ref, wq_ref, qg_ref, kg_ref, q_ref, k_ref, ckv_ref):
    cq = cq_ref[...]
    cqn = cq * lax.rsqrt(jnp.mean(cq * cq, axis=-1, keepdims=True) + EPS) * qg_ref[...]
    qf = _dot(cqn.astype(BF16), wq_ref[...])
    qtab = qtab_ref[...]
    for h in range(MLA_H):
        q_ref[:, h * Q_SLOT:(h + 1) * Q_SLOT] = (qf[:, h * Q_SLOT:(h + 1) * Q_SLOT] * qtab).astype(BF16)
    ckv = kv_ref[:, :KV_RANK]
    ckvn = ckv * lax.rsqrt(jnp.mean(ckv * ckv, axis=-1, keepdims=True) + EPS) * kg_ref[...]
    ckv_ref[...] = ckvn
    k_ref[:, :KV_RANK] = ckvn.astype(BF16)
    t = kv_ref[:, KV_RANK:] * ktab_ref[...]
    kp = t + pltpu.roll(t, MLA_DR, 1) + pltpu.roll(t, LANE - MLA_DR, 1)
    lane = lax.broadcasted_iota(jnp.int32, kp.shape, 1)
    one = jnp.where(lane == ONE_LANE - KV_RANK, 1.0, 0.0)
    k_ref[:, KV_RANK:] = jnp.where(lane < 2 * MLA_DR, kp, one).astype(BF16)


def _attn_prep(proj, qtab, ktab, wq, qg, kg, *, layer, tm, tab_tiles):
    n = proj.shape[0]
    return pl.pallas_call(
        _attn_prep_kernel,
        grid=(n // tm,),
        in_specs=[
            pl.BlockSpec((tm, Q_RANK), lambda i: (i, BLK_CQ)),
            pl.BlockSpec((tm, 2 * LANE), lambda i: (i, BLK_KV)),
            pl.BlockSpec((tm, Q_SLOT), lambda i: (i % tab_tiles, 0)),
            pl.BlockSpec((tm, LANE), lambda i: (i % tab_tiles, 0)),
            pl.BlockSpec((None, Q_RANK, MLA_H * Q_SLOT), lambda i: (layer, 0, 0)),
            pl.BlockSpec((1, Q_RANK), lambda i: (0, 0)),
            pl.BlockSpec((1, KV_RANK), lambda i: (0, 0)),
        ],
        out_specs=[
            pl.BlockSpec((tm, MLA_H * Q_SLOT), lambda i: (i, 0)),
            pl.BlockSpec((tm, Q_SLOT), lambda i: (i, 0)),
            pl.BlockSpec((tm, KV_RANK), lambda i: (i, 0)),
        ],
        out_shape=[
            jax.ShapeDtypeStruct((n, MLA_H * Q_SLOT), BF16),
            jax.ShapeDtypeStruct((n, Q_SLOT), BF16),
            jax.ShapeDtypeStruct((n, KV_RANK), F32),
        ],
        compiler_params=_params("parallel"),
        name="attention_prep",
    )(proj, proj, qtab, ktab, wq, qg, kg)


def _attn_kernel(*refs, has_cache, heads, tk, lockstep):
    if has_cache:
        q_ref, k_ref, kc_ref, o_ref = refs
    else:
        q_ref, k_ref, o_ref = refs
    n_new = k_ref.shape[0]
    blocks = [(k_ref, j * tk, min(tk, n_new - j * tk)) for j in range(pl.cdiv(n_new, tk))]
    if has_cache:
        n_old = kc_ref.shape[0]
        blocks += [(kc_ref, j * tk, min(tk, n_old - j * tk)) for j in range(pl.cdiv(n_old, tk))]
    tq = q_ref.shape[0]
    def head(h):
        q = q_ref[:, h * Q_SLOT:(h + 1) * Q_SLOT]
        m = jnp.full((tq, 1), NEG_BIG, F32)
        acc = jnp.zeros((tq, Q_SLOT), F32)
        for ref, start, size in blocks:
            kj = ref[start:start + size, :]
            s = _dot_t(q, kj)
            yield
            m_new = jnp.maximum(m, jnp.max(s, axis=-1, keepdims=True))
            acc = acc * jnp.exp2(m - m_new) + _dot(jnp.exp2(s - m_new).astype(BF16), kj)
            m = m_new
        o_ref[:, h * KV_RANK:(h + 1) * KV_RANK] = (acc[:, :KV_RANK] / acc[:, ONE_LANE:ONE_LANE + 1]).astype(BF16)

    for h0 in range(0, heads, lockstep):
        _round_robin([head(h) for h in range(h0, h0 + lockstep)])


def _attention(q, k, kcache, *, n_seq, seq_len, tq, heads, lockstep, tk=512, layer=0):
    n = q.shape[0]
    qt = seq_len // tq
    in_specs = [
        pl.BlockSpec((tq, heads * Q_SLOT), lambda b, i, h: (b * qt + i, h)),
        pl.BlockSpec((seq_len, Q_SLOT), lambda b, i, h: (b, 0)),
    ]
    args = [q, k]
    if kcache is not None:
        in_specs.append(pl.BlockSpec((None, None, kcache.shape[2], Q_SLOT), lambda b, i, h: (b, layer, 0, 0)))
        args.append(kcache)
    return pl.pallas_call(
        functools.partial(_attn_kernel, has_cache=kcache is not None, heads=heads, tk=tk, lockstep=lockstep),
        grid=(n_seq, qt, MLA_H // heads),
        in_specs=in_specs,
        out_specs=pl.BlockSpec((tq, heads * KV_RANK), lambda b, i, h: (b * qt + i, h)),
        out_shape=jax.ShapeDtypeStruct((n, MLA_H * KV_RANK), BF16),
        compiler_params=_params("parallel", "parallel", "parallel"),
        name="attention",
    )(*args)


OUTPROJ_PARTS = 4


def _outproj_kernel(x_ref, ol_ref, mg_ref, mod_ref, wuv_ref, wo_ref, g_ref, b_ref, o_ref):
    g1 = mod_ref[0, 2:3, :]
    rows_per_part = x_ref.shape[0] // OUTPROJ_PARTS

    def part(r):
        rows = slice(r * rows_per_part, (r + 1) * rows_per_part)
        a = _dot(ol_ref[rows, :], wuv_ref[...]).astype(BF16)
        yield
        mix = _dot(a, wo_ref[:MLA_H * MLA_DV, :]) + _dot(mg_ref[rows, :], wo_ref[MLA_H * MLA_DV:, :])
        yield
        y = ALPHA * x_ref[rows, :] + g1 * mix
        o_ref[rows, :] = _layernorm(y, g_ref[...], b_ref[...])

    _round_robin([part(r) for r in range(OUTPROJ_PARTS)], skew=1)


def _outproj(x, o_lat, mg, mod, wuv, wo, g, b, *, layer, tm, tiles_per_mod):
    n = x.shape[0]
    D = D_MODEL
    return pl.pallas_call(
        _outproj_kernel,
        grid=(n // tm,),
        in_specs=[
            pl.BlockSpec((tm, D), lambda i: (i, 0)),
            pl.BlockSpec((tm, MLA_H * KV_RANK), lambda i: (i, 0)),
            pl.BlockSpec((tm, 2 * REC_W), lambda i: (i, 0)),
            pl.BlockSpec((1, 6, D), lambda i: (i // tiles_per_mod, 0, 0)),
            pl.BlockSpec((None, MLA_H * KV_RANK, MLA_H * MLA_DV), lambda i: (layer, 0, 0)),
            pl.BlockSpec((None, D, D), lambda i: (layer, 0, 0)),
            pl.BlockSpec((1, D), lambda i: (0, 0)),
            pl.BlockSpec((1, D), lambda i: (0, 0)),
        ],
        out_specs=pl.BlockSpec((tm, D), lambda i: (i, 0)),
        out_shape=jax.ShapeDtypeStruct((n, D), F32),
        compiler_params=_params("parallel"),
        name="out_projection",
    )(x, o_lat, mg, mod, wuv, wo, g, b)


FF_SUB = 256
FFN_PARTS = 2
FFN_SKEW = 3


def _ffn_kernel(x_ref, mod_ref, wi_ref, wo_ref, g_ref, b_ref, o_ref):
    sh, sc, g2 = mod_ref[0, 3:4, :], mod_ref[0, 4:5, :], mod_ref[0, 5:6, :]
    rows_per_part = x_ref.shape[0] // FFN_PARTS

    def part(r):
        rows = slice(r * rows_per_part, (r + 1) * rows_per_part)
        x = x_ref[rows, :]
        h = (x * (1.0 + sc) + sh).astype(BF16)
        acc = None
        for c0 in range(0, FF, FF_SUB):
            yield
            act = (_silu(_dot(h, wi_ref[:, c0:c0 + FF_SUB])) * _dot(h, wi_ref[:, FF + c0:FF + c0 + FF_SUB])).astype(BF16)
            upd = _dot(act, wo_ref[c0:c0 + FF_SUB, :])
            acc = upd if acc is None else acc + upd
        yield
        o_ref[rows, :] = _layernorm(ALPHA * x + g2 * acc, g_ref[...], b_ref[...])

    _round_robin([part(r) for r in range(FFN_PARTS)], skew=FFN_SKEW)


def _ffn(x, mod, w_in, w_out, g, b, *, layer, tm, tiles_per_mod):
    n = x.shape[0]
    D = D_MODEL
    return pl.pallas_call(
        _ffn_kernel,
        grid=(n // tm,),
        in_specs=[
            pl.BlockSpec((tm, D), lambda i: (i, 0)),
            pl.BlockSpec((1, 6, D), lambda i: (i // tiles_per_mod, 0, 0)),
            pl.BlockSpec((None, D, 2 * FF), lambda i: (layer, 0, 0), pipeline_mode=pl.Buffered(1)),
            pl.BlockSpec((None, FF, D), lambda i: (layer, 0, 0), pipeline_mode=pl.Buffered(1)),
            pl.BlockSpec((1, D), lambda i: (0, 0)),
            pl.BlockSpec((1, D), lambda i: (0, 0)),
        ],
        out_specs=pl.BlockSpec((tm, D), lambda i: (i, 0)),
        out_shape=jax.ShapeDtypeStruct((n, D), F32),
        compiler_params=_params("parallel"),
        name="ffn",
    )(x, mod, w_in, w_out, g, b)


def _gate_terms(gti_f, gtf_f, gti_b, gtf_b, cum3_ref):
    lane = lax.broadcasted_iota(jnp.int32, (CHUNK, LANE), 1)
    is_f = lane < N_HEADS
    ig = jnp.where(is_f, gti_f, gti_b)
    lf = _log_sigmoid(jnp.where(is_f, gtf_f, gtf_b))
    b = jnp.where(is_f, _sel_rows(cum3_ref[0], lf), _sel_rows(cum3_ref[1], lf))
    return is_f, ig, lf, b


def _hgrn_gates(fr, lb):
    f = lb + (1.0 - lb) * jax.nn.sigmoid(fr)
    kk = (1.0 - lb) * jax.nn.sigmoid(-fr)
    return kk, jnp.log(jnp.maximum(f, TINY))


def _scan_kernel(*refs, nc, group, zero_init, emit_final):
    it = iter(refs)
    mk = (next(it), next(it))
    mv = (next(it), next(it))
    gf = (next(it), next(it))
    gi = (next(it), next(it))
    gti = (next(it), next(it))
    gtf = (next(it), next(it))
    lb_ref, cum3_ref, expp_ref = next(it), next(it), next(it)
    if not zero_init:
        c0_ref, n0_ref, m0_ref, s0_ref = next(it), next(it), next(it), next(it)
    cs = (next(it), next(it))
    ns = (next(it), next(it))
    ms = (next(it), next(it))
    ss = (next(it), next(it))
    if emit_final:
        cfin_ref, nfin_ref, mfin_ref, sfin_ref = next(it), next(it), next(it), next(it)
    c_scr, n_scr, m_scr, s_scr = next(it), next(it), next(it), next(it)

    pos = pl.program_id(1)

    @pl.when(pos == 0)
    def _():
        if zero_init:
            c_scr[...] = jnp.zeros_like(c_scr)
            n_scr[...] = jnp.zeros_like(n_scr)
            m_scr[...] = jnp.zeros_like(m_scr)
            s_scr[...] = jnp.zeros_like(s_scr)
        else:
            c_scr[...] = c0_ref[...]
            n_scr[...] = n0_ref[...]
            m_scr[...] = m0_ref[...]
            s_scr[...] = s0_ref[...]

    lb = lb_ref[...]
    lane_head = lax.broadcasted_iota(jnp.int32, (HEAD_D, REC_W), 1) // HEAD_D

    def diag_blocks(full):
        out = full[(N_HEADS - 1) * HEAD_D:]
        for h in range(N_HEADS - 2, -1, -1):
            out = jnp.where(lane_head == h, full[h * HEAD_D:(h + 1) * HEAD_D], out)
        return out

    def seq_step(q):
        _, ig, lf, b = _gate_terms(gti[0][q], gtf[0][q], gti[1][q], gtf[1][q], cum3_ref)
        hg = [_hgrn_gates(gf[d][q], lb) for d in range(2)]
        bcs = [_sel_rows(cum3_ref[d], hg[d][1]) for d in range(2)]
        yield
        r = ig - b
        rmax = jnp.max(r, axis=0, keepdims=True)
        g = jnp.sum(lf, axis=0, keepdims=True)
        m = m_scr[q]
        mm = jnp.maximum(m, rmax)
        sc = jnp.exp(m - mm)
        w = jnp.exp(r - mm)
        ms[0][q] = m
        ms[1][q] = m
        m_scr[q] = g + mm
        wsc = jnp.concatenate([w, jnp.broadcast_to(sc, (16, LANE))], axis=0)
        wxs = [_expand_heads(wsc, expp_ref[d]) for d in range(2)]
        s_olds, s_upds, gls = [], [], []
        for d in range(2):
            kk, lg = hg[d]
            gl = jnp.sum(lg, axis=0, keepdims=True)
            kd = kk * jnp.exp(gl - bcs[d])
            s_old = s_scr[q, d]
            ss[d][q] = s_old.astype(BF16)
            s_olds.append(s_old)
            gls.append(gl)
            s_upds.append(_dot(gi[d][q].T.astype(BF16), kd.astype(BF16)))
        yield
        news = []
        for d in range(2):
            c_old = c_scr[q, d]
            n_old = n_scr[q, d]
            cs[d][q] = c_old.astype(BF16)
            ns[d][q] = n_old
            scx = wxs[d][CHUNK:CHUNK + 1]
            kw = mk[d][q] * (HEAD_D ** -0.5) * wxs[d][:CHUNK]
            c_upd = _dot(kw.T.astype(BF16), mv[d][q].astype(BF16))
            n_new = n_old * scx + jnp.sum(kw, axis=0, keepdims=True)
            news.append((c_old, scx, c_upd, n_new))
        yield
        for d in range(2):
            c_old, scx, c_upd, n_new = news[d]
            c_new = c_old * scx + diag_blocks(c_upd)
            s_new = s_olds[d] * jnp.exp(gls[d]) + diag_blocks(s_upds[d])
            c_scr[q, d] = c_new
            n_scr[q, d] = n_new
            s_scr[q, d] = s_new
            if emit_final:
                @pl.when(pos == nc - 1)
                def _():
                    nfin_ref[q, d] = n_new
                    for h in range(N_HEADS):
                        hs = slice(h * HEAD_D, (h + 1) * HEAD_D)
                        cfin_ref[q, d, h] = c_new[:, hs]
                        sfin_ref[q, d, h] = s_new[:, hs].T
        if emit_final:
            @pl.when(pos == nc - 1)
            def _():
                mfin_ref[q] = g + mm

    _round_robin([seq_step(q) for q in range(group)])


def _state_scan(proj, lb, consts, init, *, n_seq, nc, group, emit_final, layer=0):
    zero_init = init is None
    p4 = proj.reshape(n_seq, nc, CHUNK, PROJ_W)

    def fwd(p):
        return p

    def bwd(p):
        return nc - 1 - p

    def blk(col, w, idx):
        return pl.BlockSpec((group, None, CHUNK, w), lambda s, p: (s, idx(p), 0, col))

    in_specs, args = [], []
    for col, w in ((BLK_MK, REC_W), (BLK_MV, REC_W)):
        for idx in (fwd, bwd):
            in_specs.append(blk(col, w, idx))
            args.append(p4)
    in_specs += [blk(BLK_GFF, REC_W, fwd), blk(BLK_GFB, REC_W, bwd)]
    args += [p4, p4]
    for col, w in ((BLK_GI, REC_W), (BLK_GATE_I, LANE), (BLK_GATE_F, LANE)):
        for idx in (fwd, bwd):
            in_specs.append(blk(col, w, idx))
            args.append(p4)
    in_specs += [
        pl.BlockSpec((1, REC_W), lambda s, p: (0, 0)),
        pl.BlockSpec((2, CHUNK, 3 * CHUNK), lambda s, p: (0, 0, 0)),
        pl.BlockSpec((2, LANE, REC_W), lambda s, p: (0, 0, 0)),
    ]
    args += [lb, consts["cum3"], consts["expp"]]
    state_shapes = ((2, HEAD_D, REC_W), (2, 1, REC_W), (1, LANE), (2, HEAD_D, REC_W))
    if not zero_init:
        for shape in state_shapes:
            in_specs.append(pl.BlockSpec((group, None) + shape, lambda s, p, k=len(shape): (s, layer) + (0,) * k))
        args += list(init)

    out_specs, out_shape = [], []

    def add_out(shape, dtype):
        for idx in (fwd, bwd):
            out_specs.append(pl.BlockSpec((group, None) + shape, lambda s, p, idx=idx: (s, idx(p)) + (0,) * len(shape)))
            out_shape.append(jax.ShapeDtypeStruct((n_seq, nc) + shape, dtype))

    add_out((HEAD_D, REC_W), BF16)
    add_out((1, REC_W), F32)
    add_out((1, LANE), F32)
    add_out((HEAD_D, REC_W), BF16)
    if emit_final:
        per_head = (2, N_HEADS, HEAD_D, HEAD_D)
        for shape in (per_head, state_shapes[1], state_shapes[2], per_head):
            out_specs.append(pl.BlockSpec((group,) + shape, lambda s, p, k=len(shape): (s,) + (0,) * k))
            out_shape.append(jax.ShapeDtypeStruct((n_seq,) + shape, F32))

    outs = pl.pallas_call(
        functools.partial(_scan_kernel, nc=nc, group=group, zero_init=zero_init, emit_final=emit_final),
        grid=(n_seq // group, nc),
        in_specs=in_specs,
        out_specs=out_specs,
        out_shape=out_shape,
        scratch_shapes=[pltpu.VMEM((group,) + shape, F32) for shape in state_shapes],
        compiler_params=_params("parallel", "arbitrary"),
        name="state_scan",
    )(*args)
    return list(outs)


def _round_robin(gens, skew=0):
    results = [None] * len(gens)
    done = [False] * len(gens)
    rnd = 0
    while not all(done):
        for i, g in enumerate(gens):
            if done[i] or rnd < skew * i:
                continue
            try:
                next(g)
            except StopIteration as e:
                results[i] = e.value
                done[i] = True
        rnd += 1
    return results


def _mlstm_chunk(mq, mk, mv, mo, gti, gtf, csf, csb, nsf, nsb, msf, msb,
                 mnorm, cum3_ref, expp_ref, bd, bd2, itile, caus_ref):
    inv_d = 1.0 / HEAD_D
    is_f, ig, lf, b = _gate_terms(gti, gtf, gti, gtf, cum3_ref)
    yield
    r = ig - b
    row = lax.broadcasted_iota(jnp.int32, (CHUNK, LANE), 0)
    cm_f = r
    cm_b = r
    sh = 1
    while sh < CHUNK:
        cm_f = jnp.maximum(cm_f, jnp.where(row >= sh, pltpu.roll(cm_f, sh, 0), NEG_BIG))
        cm_b = jnp.maximum(cm_b, jnp.where(row < CHUNK - sh, pltpu.roll(cm_b, CHUNK - sh, 0), NEG_BIG))
        sh *= 2
    m_row = jnp.where(is_f[0:1], msf, msb)
    big_m = jnp.maximum(m_row, jnp.where(is_f, cm_f, cm_b))
    ws = jnp.exp(m_row - big_m)
    stacked = jnp.concatenate([big_m, ws, b + big_m, r], axis=0)

    qb = mq.astype(BF16)
    k_bd = _block_diag((mk * (HEAD_D ** -0.5)).astype(BF16), bd)
    v_bd = _block_diag(mv.astype(BF16), bd)
    s = _dot_t(qb, k_bd)
    ones3 = jnp.ones((CHUNK, 3 * CHUNK), BF16)
    ex = [_expand_heads(stacked, expp_ref[d]) for d in range(2)]
    yield
    r_bcast = [_sel_rows(ones3, ex[d][3 * CHUNK:] * itile) for d in range(2)]
    yield
    p = []
    for d in range(2):
        arg = jnp.where(caus_ref[d] > 0.0, r_bcast[d] - ex[d][:CHUNK], NEG_BIG)
        p.append(s * jnp.exp(arg))
    pv = _dot(jnp.concatenate(p, axis=0).astype(BF16), v_bd)
    qc = [_dot(qb, _block_diag(c_st, bd)) for c_st in (csf, csb)]
    all_sums = _seg_sum(jnp.concatenate([mq * nsf, p[0], mq * nsb, p[1]], axis=0), bd2)
    sums = [(all_sums[2 * d * CHUNK:(2 * d + 1) * CHUNK], all_sums[(2 * d + 1) * CHUNK:(2 * d + 2) * CHUNK])
            for d in range(2)]
    yield
    hsum = None
    for d in range(2):
        wsx = ex[d][CHUNK:2 * CHUNK]
        num = wsx * qc[d] + pv[d * CHUNK:(d + 1) * CHUNK]
        den = wsx * sums[d][0] + sums[d][1]
        hd = num / jnp.maximum(jnp.abs(den), jnp.exp(-ex[d][2 * CHUNK:3 * CHUNK]))
        hsum = hd if hsum is None else hsum + hd
    mu = _seg_sum(hsum, bd2) * inv_d
    yield
    hc = hsum - mu
    var = _seg_sum(hc * hc, bd2) * inv_d
    yield
    return hc * lax.rsqrt(var + EPS) * mnorm * jax.nn.sigmoid(mo)


def _hgrn_chunk(gq, gff, gfb, gi, gg, ssf, ssb, lb, hnorm, bd, bd2, mat3_ref, msk_ref):
    inv_d = 1.0 / HEAD_D
    qh = _silu(gq)
    gv_bd = _block_diag(gi.astype(BF16), bd)
    gates = [_hgrn_gates(fr, lb) for fr in (gff, gfb)]
    args = [_sel_rows(mat3_ref[d], gates[d][1]) for d in range(2)]
    yield
    a_tot = None
    inter = None
    for d, s_st in enumerate((ssf, ssb)):
        kk = gates[d][0]
        e = jnp.exp(args[d])

        def eb(i):
            return e[i * CHUNK:(i + 1) * CHUNK]

        t_in = _dot_t((qh * eb(0)).astype(BF16), _block_diag(s_st, bd))
        inter = t_in if inter is None else inter + t_in
        stages = (
            (kk * eb(1), [qh * eb(2), qh * eb(3), qh * eb(4)]),
            (kk * eb(5), [qh * eb(6), qh * eb(7), qh * eb(8)]),
            (kk, [qh * eb(9), qh * eb(10), qh * eb(11), qh]),
        )
        outs = [_dot_t(jnp.concatenate(q_list, axis=0).astype(BF16), _block_diag(k_st.astype(BF16), bd))
                for k_st, q_list in stages]
        yield
        mi = 0
        for out, (_, q_list) in zip(outs, stages):
            for j in range(len(q_list)):
                term = msk_ref[d, mi] * out[j * CHUNK:(j + 1) * CHUNK]
                a_tot = term if a_tot is None else a_tot + term
                mi += 1
    o = _dot(a_tot.astype(BF16), gv_bd) + inter
    yield
    ms = _seg_sum(o * o, bd2) * inv_d
    yield
    return o * lax.rsqrt(ms + EPS) * hnorm * _silu(gg)


def _mixer_kernel(*refs, chunks):
    mq, mk, mv, mo, gq, gff, gfb, gi, gg, gti, gtf = refs[:11]
    csf, csb, nsf, nsb, msf, msb, ssf, ssb = refs[11:19]
    lb_ref, mnorm_ref, hnorm_ref, cum3_ref, expp_ref, bd_ref, bd2_ref, itile_ref, caus_ref, mat3_ref, msk_ref = refs[19:30]
    o_ref = refs[30]
    bd, bd2, itile = bd_ref[...], bd2_ref[...], itile_ref[...]
    lb, mnorm, hnorm = lb_ref[...], mnorm_ref[...], hnorm_ref[...]
    gens = []
    for c in range(chunks):
        rows = slice(c * CHUNK, (c + 1) * CHUNK)
        gens.append(_mlstm_chunk(*[r[rows, :] for r in (mq, mk, mv, mo, gti, gtf)],
                                 *[r[c] for r in (csf, csb, nsf, nsb, msf, msb)],
                                 mnorm, cum3_ref, expp_ref, bd, bd2, itile, caus_ref))
        gens.append(_hgrn_chunk(*[r[rows, :] for r in (gq, gff, gfb, gi, gg)], ssf[c], ssb[c],
                                lb, hnorm, bd, bd2, mat3_ref, msk_ref))
    outs = _round_robin(gens)
    for c in range(chunks):
        rows = slice(c * CHUNK, (c + 1) * CHUNK)
        o_ref[rows, :REC_W] = outs[2 * c].astype(BF16)
        o_ref[rows, REC_W:] = outs[2 * c + 1].astype(BF16)


def _mixer_outputs(proj, states, lb, mnorm, hnorm, consts, *, chunks):
    n = proj.shape[0]
    rows = chunks * CHUNK

    def blk(col, w):
        return pl.BlockSpec((rows, w), lambda c: (c, col))

    in_specs = [blk(c, REC_W) for c in (BLK_MQ, BLK_MK, BLK_MV, BLK_MO, BLK_GQ, BLK_GFF, BLK_GFB, BLK_GI, BLK_GG)]
    in_specs += [blk(BLK_GATE_I, LANE), blk(BLK_GATE_F, LANE)]
    args = [proj] * 11
    for arr in states:
        shape = arr.shape[2:]
        per_seq = arr.shape[1] // chunks
        in_specs.append(pl.BlockSpec((None, chunks) + shape,
                                     lambda c, k=len(shape), per_seq=per_seq: (c // per_seq, c % per_seq) + (0,) * k))
        args.append(arr)

    def whole(arr):
        return pl.BlockSpec(arr.shape, lambda c, k=arr.ndim: (0,) * k)

    for arr in (lb, mnorm, hnorm, consts["cum3"], consts["expp"], consts["bd"], consts["bd2"], consts["itile"],
                consts["caus"], consts["mat3"], consts["msk"]):
        in_specs.append(whole(arr))
        args.append(arr)
    return pl.pallas_call(
        functools.partial(_mixer_kernel, chunks=chunks),
        grid=(n // rows,),
        in_specs=in_specs,
        out_specs=pl.BlockSpec((rows, 2 * REC_W), lambda c: (c, 0)),
        out_shape=jax.ShapeDtypeStruct((n, 2 * REC_W), BF16),
        compiler_params=_params("parallel"),
        name="mixer_outputs",
    )(*args)


def _rot_perm():
    q = MLA_DR // 4
    idx = np.concatenate([np.arange(q, 2 * q), np.arange(0, q), np.arange(3 * q, 4 * q), np.arange(2 * q, 3 * q)])
    sign = np.concatenate([-np.ones(q), np.ones(q), -np.ones(q), np.ones(q)]).astype(np.float32)
    return idx, sign


def _pad_cols(a, width):
    return jnp.pad(a, [(0, 0)] * (a.ndim - 1) + [(0, width - a.shape[-1])])


def _layout_in_proj(w_in, b_in):
    off = np.concatenate([[0], np.cumsum(IN_SIZES)])
    idx, sign = _rot_perm()

    def lay(a):
        def cols(i, j=None):
            return a[..., int(off[i]):int(off[(i if j is None else j) + 1])]

        kpe = cols(2)
        return jnp.concatenate([
            cols(0), cols(1),
            _pad_cols(jnp.concatenate([kpe, kpe[..., idx] * sign], axis=-1), LANE),
            cols(3, 6), cols(11, 15),
            _pad_cols(jnp.concatenate([cols(7), cols(8)], axis=-1), LANE),
            _pad_cols(jnp.concatenate([cols(9), cols(10)], axis=-1), LANE),
        ], axis=-1)

    return lay(w_in).astype(BF16), lay(b_in)[:, None, :]


def _layout_query_weights(w_uq, w_abs):
    idx, sign = _rot_perm()
    w = w_uq.reshape(DEPTH, Q_RANK, MLA_H, MLA_DN + MLA_DR)
    pe = w[..., MLA_DN:]
    slot = jnp.concatenate([jnp.transpose(w_abs, (0, 2, 1, 3)), pe, pe[..., idx] * sign], axis=-1)
    return _pad_cols(slot, Q_SLOT).reshape(DEPTH, Q_RANK, MLA_H * Q_SLOT).astype(BF16)


def _layout_value_weights(w_ukv):
    w = w_ukv.reshape(DEPTH, KV_RANK, MLA_H, MLA_DN + MLA_DV)[..., MLA_DN:]
    eye = jnp.eye(MLA_H, dtype=w.dtype)
    bdw = jnp.einsum("lrhe,hg->lhrge", w, eye)
    return bdw.reshape(DEPTH, MLA_H * KV_RANK, MLA_H * MLA_DV).astype(BF16)


def _rope_tables(n):
    n_rows = n // GRID_W
    row = jnp.repeat(jnp.arange(n_rows, dtype=F32), GRID_W)
    col = jnp.tile(jnp.arange(GRID_W, dtype=F32), n_rows)
    half = MLA_DR // 2
    freqs = ROPE_BASE ** (-jnp.arange(half // 2, dtype=F32) * (2.0 / half))
    ar = row[:, None] * freqs
    ac = col[:, None] * freqs
    ang = jnp.concatenate([ar, ar, ac, ac], -1)
    return jnp.cos(ang), jnp.sin(ang)


def kernel(x_prompt, x_sample, cache_mla_ckv, cache_mla_kpe, state_mlstm_C, state_mlstm_n, state_mlstm_m, state_hgrn_S, c, c_ctx, w_mod, b_mod, w_in, b_in, mla_q_norm, w_uq, mla_kv_norm, w_ukv, mlstm_norm, hgrn_lb_logits, hgrn_norm, w_out, ln1_g, ln1_b, w_ffn_in, w_ffn_out, ln2_g, ln2_b):
    B, T, D = x_prompt.shape
    Bs, Ts, _ = x_sample.shape
    past = cache_mla_ckv.shape[2]
    consts = _recurrent_constants()

    cvec = jnp.concatenate([c_ctx[None, :], c, jnp.zeros((8 - 1 - Bs, D), F32)], axis=0)
    mod = jnp.transpose(_modulation(cvec, w_mod, b_mod), (0, 2, 1, 3))
    lbs = _hgrn_lower_bounds(hgrn_lb_logits)

    w_in_p, b_in_p = _layout_in_proj(w_in, b_in)
    wq4 = w_uq.reshape(DEPTH, Q_RANK, MLA_H, MLA_DN + MLA_DR)
    wkv4 = w_ukv.reshape(DEPTH, KV_RANK, MLA_H, MLA_DN + MLA_DV)
    w_abs = _absorbed_query_weights(jnp.transpose(wq4[..., :MLA_DN], (0, 2, 1, 3)),
                                    jnp.transpose(wkv4[..., :MLA_DN], (0, 2, 1, 3)))
    wq_p = _layout_query_weights(w_uq, w_abs)
    wuv_p = _layout_value_weights(w_ukv)
    w_out_b = w_out.astype(BF16)
    w_ffn_in_b = w_ffn_in.astype(BF16)
    w_ffn_out_b = w_ffn_out.astype(BF16)

    scale = (MLA_DN + MLA_DR) ** -0.5 * float(np.log2(np.e))
    cos, sin = _rope_tables(Ts)
    qtab_s = scale * jnp.concatenate([jnp.ones((Ts, KV_RANK), F32), cos, sin, jnp.zeros((Ts, Q_SLOT - KV_RANK - 2 * MLA_DR), F32)], axis=1)
    ktab_s = jnp.concatenate([cos, sin, jnp.zeros((Ts, LANE - 2 * MLA_DR), F32)], axis=1)
    TM_C = 256
    lane_q = np.arange(Q_SLOT)
    qtab_c = jnp.asarray(np.tile(scale * (lane_q < KV_RANK + MLA_DR), (TM_C, 1)), F32)
    ktab_c = jnp.asarray(np.tile(np.arange(LANE) < MLA_DR, (TM_C, 1)), F32)

    kcache = jnp.concatenate([cache_mla_ckv, cache_mla_kpe, cache_mla_kpe,
                              jnp.ones(cache_mla_kpe.shape[:-1] + (1,), F32),
                              jnp.zeros(cache_mla_kpe.shape[:-1] + (Q_SLOT - ONE_LANE - 1,), F32)], axis=-1).astype(BF16)
    c0 = jnp.transpose(state_mlstm_C, (0, 1, 2, 4, 3, 5)).reshape(Bs, DEPTH, 2, HEAD_D, REC_W)
    n0 = state_mlstm_n.reshape(Bs, DEPTH, 2, 1, REC_W)
    m0 = _pad_cols(state_mlstm_m.reshape(Bs, DEPTH, 1, 2 * N_HEADS), LANE)
    s0 = jnp.transpose(state_hgrn_S, (0, 1, 2, 5, 3, 4)).reshape(Bs, DEPTH, 2, HEAD_D, REC_W)

    xp = x_prompt.reshape(B * T, D)
    xs = x_sample.reshape(Bs * Ts, D)
    nc_c, nc_s = T // CHUNK, Ts // CHUNK
    TM = 512
    TM_F = 1024
    ckv_l, kpe_l, fin_l = [], [], []
    for l in range(DEPTH):
        lb = lbs[l][None, :]
        mnorm, hnorm = mlstm_norm[l][None, :], hgrn_norm[l][None, :]
        qg, kg = mla_q_norm[l][None, :], mla_kv_norm[l][None, :]
        g1, b1, g2, b2 = ln1_g[l][None, :], ln1_b[l][None, :], ln2_g[l][None, :], ln2_b[l][None, :]
        mod_c, mod_s = mod[l, 0:1], mod[l, 1:1 + Bs]

        proj = _inproj(xp, mod_c, w_in_p, b_in_p, layer=l, tm=TM, tiles_per_mod=B * T // TM)
        qa, ka, ckvn = _attn_prep(proj, qtab_c, ktab_c, wq_p, qg, kg, layer=l, tm=TM_C, tab_tiles=1)
        o_lat = _attention(qa, ka, None, n_seq=B, seq_len=T, tq=T, heads=MLA_H, lockstep=MLA_H)
        scan = _state_scan(proj, lb, consts, None, n_seq=B, nc=nc_c, group=8, emit_final=True)
        mg = _mixer_outputs(proj, scan[:8], lb, mnorm, hnorm, consts, chunks=4)
        x1 = _outproj(xp, o_lat, mg, mod_c, wuv_p, w_out_b, g1, b1, layer=l, tm=TM_F, tiles_per_mod=B * T // TM_F)
        xp = _ffn(x1, mod_c, w_ffn_in_b, w_ffn_out_b, g2, b2, layer=l, tm=TM_F, tiles_per_mod=B * T // TM_F)
        ckv_l.append(ckvn.reshape(B, T, KV_RANK))
        kpe_l.append(proj[:, 3 * LANE:3 * LANE + MLA_DR].reshape(B, T, MLA_DR))
        fin_l.append(scan[8:])

        proj = _inproj(xs, mod_s, w_in_p, b_in_p, layer=l, tm=TM, tiles_per_mod=Ts // TM)
        qa, ka, _ = _attn_prep(proj, qtab_s, ktab_s, wq_p, qg, kg, layer=l, tm=TM, tab_tiles=Ts // TM)
        o_lat = _attention(qa, ka, kcache, n_seq=Bs, seq_len=Ts, tq=256, heads=8, lockstep=1, layer=l)
        scan = _state_scan(proj, lb, consts, (c0, n0, m0, s0), n_seq=Bs, nc=nc_s, group=Bs, emit_final=False, layer=l)
        mg = _mixer_outputs(proj, scan[:8], lb, mnorm, hnorm, consts, chunks=8)
        x1 = _outproj(xs, o_lat, mg, mod_s, wuv_p, w_out_b, g1, b1, layer=l, tm=TM_F, tiles_per_mod=Ts // TM_F)
        xs = _ffn(x1, mod_s, w_ffn_in_b, w_ffn_out_b, g2, b2, layer=l, tm=TM_F, tiles_per_mod=Ts // TM_F)

    new_ckv = jnp.stack(ckv_l, axis=1)
    new_kpe = jnp.stack(kpe_l, axis=1)
    new_c = jnp.stack([f[0] for f in fin_l], axis=1)
    new_n = jnp.stack([f[1].reshape(B, 2, N_HEADS, HEAD_D) for f in fin_l], axis=1)
    new_m = jnp.stack([f[2][:, 0, :2 * N_HEADS].reshape(B, 2, N_HEADS) for f in fin_l], axis=1)
    new_s = jnp.stack([f[3] for f in fin_l], axis=1)
    return (xp.reshape(B, T, D), xs.reshape(Bs, Ts, D), new_ckv, new_kpe, new_c, new_n, new_m, new_s)
```

```python
import functools

import numpy as np
import jax
import jax.numpy as jnp
from jax import lax
from jax.experimental import pallas as pl
from jax.experimental.pallas import tpu as pltpu

F32 = jnp.float32
BF16 = jnp.bfloat16

D_MODEL = 1024
DEPTH = 4
GRID_W = 64
MLA_H = 8
MLA_DN = 64
MLA_DR = 32
MLA_DV = 64
Q_RANK = 256
KV_RANK = 128
N_HEADS = 4
HEAD_D = 64
REC_W = N_HEADS * HEAD_D
FF = 2816
CHUNK = 64
ROPE_BASE = 10000.0
ALPHA = (2 * DEPTH) ** 0.25
EPS = 1e-6
TINY = 1e-30
NEG_BIG = -1e30
IN_SIZES = (256, 128, 32, 256, 256, 256, 256, 4, 4, 4, 4, 256, 256, 256, 256, 256)
LANE = 128
Q_SLOT = 2 * LANE
ONE_LANE = KV_RANK + 2 * MLA_DR
PROJ_W = 24 * LANE
VMEM_LIMIT = 56 * 1024 * 1024

BLK_CQ, BLK_KV, BLK_MQ, BLK_MK, BLK_MV, BLK_MO, BLK_GQ, BLK_GFF, BLK_GFB, BLK_GI, BLK_GG = range(11)
BLK_GATE_I, BLK_GATE_F = 22, 23


def _params(*sem):
    return pltpu.CompilerParams(dimension_semantics=sem, vmem_limit_bytes=VMEM_LIMIT)


def _split3(x):
    x1 = x.astype(BF16)
    r1 = x - x1.astype(F32)
    x2 = r1.astype(BF16)
    x3 = (r1 - x2.astype(F32)).astype(BF16)
    return x1, x2, x3


def _dot(a, b):
    return jnp.dot(a, b, preferred_element_type=F32)


def _dot_t(a, b):
    return lax.dot_general(a, b, (((1,), (1,)), ((), ())), preferred_element_type=F32)


def _sel_rows(sel3, x):
    return _dot(sel3, jnp.concatenate(_split3(x), axis=0))


def _expand_heads(x, expp):
    x1 = x.astype(BF16).astype(F32)
    r1 = x - x1
    x2 = r1.astype(BF16).astype(F32)
    x3 = (r1 - x2).astype(BF16).astype(F32)
    lane = lax.broadcasted_iota(jnp.int32, x.shape, 1)
    n = 2 * N_HEADS
    packed = jnp.where(lane < n, x1, jnp.where(lane < 2 * n, pltpu.roll(x2, n, 1), pltpu.roll(x3, 2 * n, 1)))
    return _dot(packed.astype(BF16), expp)


def _seg_sum(x, bd2):
    x1 = x.astype(BF16)
    x2 = (x - x1.astype(F32)).astype(BF16)
    return _dot(jnp.concatenate([x1, x2], axis=1), bd2)


def _block_diag(x_bf16, bd):
    return jnp.concatenate([x_bf16] * N_HEADS, axis=0) * bd


def _log_sigmoid(x):
    return jnp.minimum(x, 0.0) - jnp.log1p(jnp.exp(-jnp.abs(x)))


def _silu(x):
    return x * jax.nn.sigmoid(x)


def _layernorm(y, g, b):
    mu = jnp.mean(y, axis=-1, keepdims=True)
    yc = y - mu
    var = jnp.mean(yc * yc, axis=-1, keepdims=True)
    return yc * lax.rsqrt(var + EPS) * g + b


def _recurrent_constants():
    L = CHUNK
    t = np.arange(L)
    tri = (t[None, :] <= t[:, None]).astype(np.float32)
    cum = np.stack([tri, tri.T])
    cum3 = np.tile(cum, (1, 1, 3))

    lane = np.arange(REC_W)
    expp = np.zeros((2, LANE, REC_W), np.float32)
    for d in range(2):
        for j in range(3):
            expp[d, 2 * N_HEADS * j + d * N_HEADS + lane // HEAD_D, lane] = 1.0

    bd = (lane[:, None] // HEAD_D == lane[None, :] // HEAD_D).astype(np.float32)
    bd2 = np.tile(bd, (2, 1))
    itile = (t[:, None] == (lane % HEAD_D)[None, :]).astype(np.float32)
    s_of_lane = lane % HEAD_D
    caus = np.stack([(s_of_lane[None, :] <= t[:, None]), (s_of_lane[None, :] >= t[:, None])]).astype(np.float32)

    def stage(G, g):
        ka = np.zeros((L, L), np.float32)
        qa = np.zeros((3, L, L), np.float32)
        msk = np.zeros((3, L, L), np.float32)
        for s in range(L):
            e = (s // g) * g + g - 1
            ka[s, s + 1:e + 1] = 1.0
        for tt in range(L):
            p = (tt % G) // g
            for j in range(3):
                if p > j:
                    E = (tt // G) * G + (j + 1) * g - 1
                    qa[j, tt, E + 1:tt + 1] = 1.0
                    for s in range(L):
                        if s // G == tt // G and (s % G) // g == j:
                            msk[j, tt, s] = 1.0
        return ka, qa, msk

    mats, msks = [], []
    for d in range(2):
        perm = t if d == 0 else t[::-1]

        def mir(m):
            return m[np.ix_(perm, perm)]

        ka16, qa16, m16 = stage(64, 16)
        ka4, qa4, m4 = stage(16, 4)
        _, qa1, m1 = stage(4, 1)
        blocks = [tri, ka16, qa16[0], qa16[1], qa16[2], ka4, qa4[0], qa4[1], qa4[2]]
        mats.append(np.concatenate([mir(b) for b in blocks], axis=0))
        mk = [m16[0], m16[1], m16[2], m4[0], m4[1], m4[2], m1[0], m1[1], m1[2], np.eye(L, dtype=np.float32)]
        msks.append(np.stack([np.tile(mir(m), (1, N_HEADS)) for m in mk]))
    mat3 = np.tile(np.stack(mats), (1, 1, 3))
    msk = np.stack(msks)
    return dict(
        cum3=jnp.asarray(cum3, BF16), expp=jnp.asarray(expp, BF16),
        bd=jnp.asarray(bd, BF16),
        bd2=jnp.asarray(bd2, BF16), itile=jnp.asarray(itile, F32), caus=jnp.asarray(caus, F32),
        mat3=jnp.asarray(mat3, BF16), msk=jnp.asarray(msk, F32))


def _mod_kernel(c_ref, w_ref, b_ref, o_ref):
    a = _silu(c_ref[...]).astype(BF16)
    o_ref[...] = _dot(a, w_ref[...].astype(BF16)) + b_ref[...]


def _modulation(cvec8, w_mod, b_mod):
    D = D_MODEL
    return pl.pallas_call(
        _mod_kernel,
        grid=(DEPTH, 6),
        in_specs=[
            pl.BlockSpec((8, D), lambda l, j: (0, 0)),
            pl.BlockSpec((None, D, D), lambda l, j: (l, 0, j)),
            pl.BlockSpec((None, None, 1, D), lambda l, j: (l, j, 0, 0)),
        ],
        out_specs=pl.BlockSpec((None, None, 8, D), lambda l, j: (l, j, 0, 0)),
        out_shape=jax.ShapeDtypeStruct((DEPTH, 6, 8, D), F32),
        compiler_params=_params("parallel", "parallel"),
        name="modulation",
    )(cvec8, w_mod, b_mod.reshape(DEPTH, 6, 1, D))


def _lb_kernel(x_ref, o_ref):
    x = x_ref[...]
    e = jnp.exp(x - jnp.max(x, axis=0, keepdims=True))
    p = e / jnp.sum(e, axis=0, keepdims=True)
    acc = jnp.zeros_like(p[0:1])
    rows = []
    for l in range(DEPTH):
        acc = acc + p[l:l + 1]
        rows.append(acc - p[0:1])
    o_ref[...] = jnp.concatenate(rows, axis=0)


def _hgrn_lower_bounds(logits):
    return pl.pallas_call(
        _lb_kernel, out_shape=jax.ShapeDtypeStruct(logits.shape, F32), name="hgrn_lower_bounds",
    )(logits.astype(F32))


def _absorb_kernel(a_ref, b_ref, o_ref):
    a1, a2, a3 = _split3(a_ref[...])
    b1, b2, b3 = _split3(b_ref[...])
    o_ref[...] = (_dot_t(a1, b1) + _dot_t(a1, b2) + _dot_t(a2, b1)
                  + _dot_t(a1, b3) + _dot_t(a2, b2) + _dot_t(a3, b1))


def _absorbed_query_weights(wq_nope, wk_nope):
    return pl.pallas_call(
        _absorb_kernel,
        grid=(DEPTH, MLA_H),
        in_specs=[
            pl.BlockSpec((None, None, Q_RANK, MLA_DN), lambda l, h: (l, h, 0, 0)),
            pl.BlockSpec((None, None, KV_RANK, MLA_DN), lambda l, h: (l, h, 0, 0)),
        ],
        out_specs=pl.BlockSpec((None, None, Q_RANK, KV_RANK), lambda l, h: (l, h, 0, 0)),
        out_shape=jax.ShapeDtypeStruct((DEPTH, MLA_H, Q_RANK, KV_RANK), F32),
        compiler_params=_params("parallel", "parallel"),
        name="absorb_query_weights",
    )(wq_nope, wk_nope)


def _inproj_kernel(x_ref, mod_ref, w_ref, b_ref, o_ref):
    sh, sc = mod_ref[0, 0:1, :], mod_ref[0, 1:2, :]
    h = (x_ref[...] * (1.0 + sc) + sh).astype(BF16)
    o_ref[...] = _dot(h, w_ref[...]) + b_ref[...]


def _inproj(x, mod, w, b, *, layer, tm, tiles_per_mod):
    n = x.shape[0]
    return pl.pallas_call(
        _inproj_kernel,
        grid=(n // tm,),
        in_specs=[
            pl.BlockSpec((tm, D_MODEL), lambda i: (i, 0)),
            pl.BlockSpec((1, 6, D_MODEL), lambda i: (i // tiles_per_mod, 0, 0)),
            pl.BlockSpec((None, D_MODEL, PROJ_W), lambda i: (layer, 0, 0)),
            pl.BlockSpec((None, 1, PROJ_W), lambda i: (layer, 0, 0)),
        ],
        out_specs=pl.BlockSpec((tm, PROJ_W), lambda i: (i, 0)),
        out_shape=jax.ShapeDtypeStruct((n, PROJ_W), F32),
        compiler_params=_params("parallel"),
        name="in_projection",
    )(x, mod, w, b)


def _attn_prep_kernel(cq_ref, kv_ref, qtab_ref, ktab_ref, wq_ref, qg_ref, kg_ref, q_ref, k_ref, ckv_ref):
    cq = cq_ref[...]
    cqn = cq * lax.rsqrt(jnp.mean(cq * cq, axis=-1, keepdims=True) + EPS) * qg_ref[...]
    qf = _dot(cqn.astype(BF16), wq_ref[...])
    qtab = qtab_ref[...]
    for h in range(MLA_H):
        q_ref[:, h * Q_SLOT:(h + 1) * Q_SLOT] = (qf[:, h * Q_SLOT:(h + 1) * Q_SLOT] * qtab).astype(BF16)
    ckv = kv_ref[:, :KV_RANK]
    ckvn = ckv * lax.rsqrt(jnp.mean(ckv * ckv, axis=-1, keepdims=True) + EPS) * kg_ref[...]
    ckv_ref[...] = ckvn
    k_ref[:, :KV_RANK] = ckvn.astype(BF16)
    t = kv_ref[:, KV_RANK:] * ktab_ref[...]
    kp = t + pltpu.roll(t, MLA_DR, 1) + pltpu.roll(t, LANE - MLA_DR, 1)
    lane = lax.broadcasted_iota(jnp.int32, kp.shape, 1)
    one = jnp.where(lane == ONE_LANE - KV_RANK, 1.0, 0.0)
    k_ref[:, KV_RANK:] = jnp.where(lane < 2 * MLA_DR, kp, one).astype(BF16)


def _attn_prep(proj, qtab, ktab, wq, qg, kg, *, layer, tm, tab_tiles):
    n = proj.shape[0]
    return pl.pallas_call(
        _attn_prep_kernel,
        grid=(n // tm,),
        in_specs=[
            pl.BlockSpec((tm, Q_RANK), lambda i: (i, BLK_CQ)),
            pl.BlockSpec((tm, 2 * LANE), lambda i: (i, BLK_KV)),
            pl.BlockSpec((tm, Q_SLOT), lambda i: (i % tab_tiles, 0)),
            pl.BlockSpec((tm, LANE), lambda i: (i % tab_tiles, 0)),
            pl.BlockSpec((None, Q_RANK, MLA_H * Q_SLOT), lambda i: (layer, 0, 0)),
            pl.BlockSpec((1, Q_RANK), lambda i: (0, 0)),
            pl.BlockSpec((1, KV_RANK), lambda i: (0, 0)),
        ],
        out_specs=[
            pl.BlockSpec((tm, MLA_H * Q_SLOT), lambda i: (i, 0)),
            pl.BlockSpec((tm, Q_SLOT), lambda i: (i, 0)),
            pl.BlockSpec((tm, KV_RANK), lambda i: (i, 0)),
        ],
        out_shape=[
            jax.ShapeDtypeStruct((n, MLA_H * Q_SLOT), BF16),
            jax.ShapeDtypeStruct((n, Q_SLOT), BF16),
            jax.ShapeDtypeStruct((n, KV_RANK), F32),
        ],
        compiler_params=_params("parallel"),
        name="attention_prep",
    )(proj, proj, qtab, ktab, wq, qg, kg)


def _attn_kernel(*refs, has_cache, heads, tk, lockstep):
    if has_cache:
        q_ref, k_ref, kc_ref, o_ref = refs
    else:
        q_ref, k_ref, o_ref = refs
    n_new = k_ref.shape[0]
    blocks = [(k_ref, j * tk, min(tk, n_new - j * tk)) for j in range(pl.cdiv(n_new, tk))]
    if has_cache:
        n_old = kc_ref.shape[0]
        blocks += [(kc_ref, j * tk, min(tk, n_old - j * tk)) for j in range(pl.cdiv(n_old, tk))]
    tq = q_ref.shape[0]
    def head(h):
        q = q_ref[:, h * Q_SLOT:(h + 1) * Q_SLOT]
        m = jnp.full((tq, 1), NEG_BIG, F32)
        acc = jnp.zeros((tq, Q_SLOT), F32)
        for ref, start, size in blocks:
            kj = ref[start:start + size, :]
            s = _dot_t(q, kj)
            yield
            m_new = jnp.maximum(m, jnp.max(s, axis=-1, keepdims=True))
            acc = acc * jnp.exp2(m - m_new) + _dot(jnp.exp2(s - m_new).astype(BF16), kj)
            m = m_new
        o_ref[:, h * KV_RANK:(h + 1) * KV_RANK] = (acc[:, :KV_RANK] / acc[:, ONE_LANE:ONE_LANE + 1]).astype(BF16)

    for h0 in range(0, heads, lockstep):
        _round_robin([head(h) for h in range(h0, h0 + lockstep)])


def _attention(q, k, kcache, *, n_seq, seq_len, tq, heads, lockstep, tk=512, layer=0):
    n = q.shape[0]
    qt = seq_len // tq
    in_specs = [
        pl.BlockSpec((tq, heads * Q_SLOT), lambda b, i, h: (b * qt + i, h)),
        pl.BlockSpec((seq_len, Q_SLOT), lambda b, i, h: (b, 0)),
    ]
    args = [q, k]
    if kcache is not None:
        in_specs.append(pl.BlockSpec((None, None, kcache.shape[2], Q_SLOT), lambda b, i, h: (b, layer, 0, 0)))
        args.append(kcache)
    return pl.pallas_call(
        functools.partial(_attn_kernel, has_cache=kcache is not None, heads=heads, tk=tk, lockstep=lockstep),
        grid=(n_seq, qt, MLA_H // heads),
        in_specs=in_specs,
        out_specs=pl.BlockSpec((tq, heads * KV_RANK), lambda b, i, h: (b * qt + i, h)),
        out_shape=jax.ShapeDtypeStruct((n, MLA_H * KV_RANK), BF16),
        compiler_params=_params("parallel", "parallel", "parallel"),
        name="attention",
    )(*args)


OUTPROJ_PARTS = 4


def _outproj_kernel(x_ref, ol_ref, mg_ref, mod_ref, wuv_ref, wo_ref, g_ref, b_ref, o_ref):
    g1 = mod_ref[0, 2:3, :]
    rows_per_part = x_ref.shape[0] // OUTPROJ_PARTS

    def part(r):
        rows = slice(r * rows_per_part, (r + 1) * rows_per_part)
        a = _dot(ol_ref[rows, :], wuv_ref[...]).astype(BF16)
        yield
        mix = _dot(a, wo_ref[:MLA_H * MLA_DV, :]) + _dot(mg_ref[rows, :], wo_ref[MLA_H * MLA_DV:, :])
        yield
        y = ALPHA * x_ref[rows, :] + g1 * mix
        o_ref[rows, :] = _layernorm(y, g_ref[...], b_ref[...])

    _round_robin([part(r) for r in range(OUTPROJ_PARTS)], skew=1)


def _outproj(x, o_lat, mg, mod, wuv, wo, g, b, *, layer, tm, tiles_per_mod):
    n = x.shape[0]
    D = D_MODEL
    return pl.pallas_call(
        _outproj_kernel,
        grid=(n // tm,),
        in_specs=[
            pl.BlockSpec((tm, D), lambda i: (i, 0)),
            pl.BlockSpec((tm, MLA_H * KV_RANK), lambda i: (i, 0)),
            pl.BlockSpec((tm, 2 * REC_W), lambda i: (i, 0)),
            pl.BlockSpec((1, 6, D), lambda i: (i // tiles_per_mod, 0, 0)),
            pl.BlockSpec((None, MLA_H * KV_RANK, MLA_H * MLA_DV), lambda i: (layer, 0, 0)),
            pl.BlockSpec((None, D, D), lambda i: (layer, 0, 0)),
            pl.BlockSpec((1, D), lambda i: (0, 0)),
            pl.BlockSpec((1, D), lambda i: (0, 0)),
        ],
        out_specs=pl.BlockSpec((tm, D), lambda i: (i, 0)),
        out_shape=jax.ShapeDtypeStruct((n, D), F32),
        compiler_params=_params("parallel"),
        name="out_projection",
    )(x, o_lat, mg, mod, wuv, wo, g, b)


FF_SUB = 256
FFN_PARTS = 2
FFN_SKEW = 3


def _ffn_kernel(x_ref, mod_ref, wi_ref, wo_ref, g_ref, b_ref, o_ref):
    sh, sc, g2 = mod_ref[0, 3:4, :], mod_ref[0, 4:5, :], mod_ref[0, 5:6, :]
    rows_per_part = x_ref.shape[0] // FFN_PARTS

    def part(r):
        rows = slice(r * rows_per_part, (r + 1) * rows_per_part)
        x = x_ref[rows, :]
        h = (x * (1.0 + sc) + sh).astype(BF16)
        acc = None
        for c0 in range(0, FF, FF_SUB):
            yield
            act = (_silu(_dot(h, wi_ref[:, c0:c0 + FF_SUB])) * _dot(h, wi_ref[:, FF + c0:FF + c0 + FF_SUB])).astype(BF16)
            upd = _dot(act, wo_ref[c0:c0 + FF_SUB, :])
            acc = upd if acc is None else acc + upd
        yield
        o_ref[rows, :] = _layernorm(ALPHA * x + g2 * acc, g_ref[...], b_ref[...])

    _round_robin([part(r) for r in range(FFN_PARTS)], skew=FFN_SKEW)


def _ffn(x, mod, w_in, w_out, g, b, *, layer, tm, tiles_per_mod):
    n = x.shape[0]
    D = D_MODEL
    return pl.pallas_call(
        _ffn_kernel,
        grid=(n // tm,),
        in_specs=[
            pl.BlockSpec((tm, D), lambda i: (i, 0)),
            pl.BlockSpec((1, 6, D), lambda i: (i // tiles_per_mod, 0, 0)),
            pl.BlockSpec((None, D, 2 * FF), lambda i: (layer, 0, 0), pipeline_mode=pl.Buffered(1)),
            pl.BlockSpec((None, FF, D), lambda i: (layer, 0, 0), pipeline_mode=pl.Buffered(1)),
            pl.BlockSpec((1, D), lambda i: (0, 0)),
            pl.BlockSpec((1, D), lambda i: (0, 0)),
        ],
        out_specs=pl.BlockSpec((tm, D), lambda i: (i, 0)),
        out_shape=jax.ShapeDtypeStruct((n, D), F32),
        compiler_params=_params("parallel"),
        name="ffn",
    )(x, mod, w_in, w_out, g, b)


def _gate_terms(gti_f, gtf_f, gti_b, gtf_b, cum3_ref):
    lane = lax.broadcasted_iota(jnp.int32, (CHUNK, LANE), 1)
    is_f = lane < N_HEADS
    ig = jnp.where(is_f, gti_f, gti_b)
    lf = _log_sigmoid(jnp.where(is_f, gtf_f, gtf_b))
    b = jnp.where(is_f, _sel_rows(cum3_ref[0], lf), _sel_rows(cum3_ref[1], lf))
    return is_f, ig, lf, b


def _hgrn_gates(fr, lb):
    f = lb + (1.0 - lb) * jax.nn.sigmoid(fr)
    kk = (1.0 - lb) * jax.nn.sigmoid(-fr)
    return kk, jnp.log(jnp.maximum(f, TINY))


def _scan_kernel(*refs, nc, group, zero_init, emit_final):
    it = iter(refs)
    mk = (next(it), next(it))
    mv = (next(it), next(it))
    gf = (next(it), next(it))
    gi = (next(it), next(it))
    gti = (next(it), next(it))
    gtf = (next(it), next(it))
    lb_ref, cum3_ref, expp_ref = next(it), next(it), next(it)
    if not zero_init:
        c0_ref, n0_ref, m0_ref, s0_ref = next(it), next(it), next(it), next(it)
    cs = (next(it), next(it))
    ns = (next(it), next(it))
    ms = (next(it), next(it))
    ss = (next(it), next(it))
    if emit_final:
        cfin_ref, nfin_ref, mfin_ref, sfin_ref = next(it), next(it), next(it), next(it)
    c_scr, n_scr, m_scr, s_scr = next(it), next(it), next(it), next(it)

    pos = pl.program_id(1)

    @pl.when(pos == 0)
    def _():
        if zero_init:
            c_scr[...] = jnp.zeros_like(c_scr)
            n_scr[...] = jnp.zeros_like(n_scr)
            m_scr[...] = jnp.zeros_like(m_scr)
            s_scr[...] = jnp.zeros_like(s_scr)
        else:
            c_scr[...] = c0_ref[...]
            n_scr[...] = n0_ref[...]
            m_scr[...] = m0_ref[...]
            s_scr[...] = s0_ref[...]

    lb = lb_ref[...]
    lane_head = lax.broadcasted_iota(jnp.int32, (HEAD_D, REC_W), 1) // HEAD_D

    def diag_blocks(full):
        out = full[(N_HEADS - 1) * HEAD_D:]
        for h in range(N_HEADS - 2, -1, -1):
            out = jnp.where(lane_head == h, full[h * HEAD_D:(h + 1) * HEAD_D], out)
        return out

    def seq_step(q):
        _, ig, lf, b = _gate_terms(gti[0][q], gtf[0][q], gti[1][q], gtf[1][q], cum3_ref)
        hg = [_hgrn_gates(gf[d][q], lb) for d in range(2)]
        bcs = [_sel_rows(cum3_ref[d], hg[d][1]) for d in range(2)]
        yield
        r = ig - b
        rmax = jnp.max(r, axis=0, keepdims=True)
        g = jnp.sum(lf, axis=0, keepdims=True)
        m = m_scr[q]
        mm = jnp.maximum(m, rmax)
        sc = jnp.exp(m - mm)
        w = jnp.exp(r - mm)
        ms[0][q] = m
        ms[1][q] = m
        m_scr[q] = g + mm
        wsc = jnp.concatenate([w, jnp.broadcast_to(sc, (16, LANE))], axis=0)
        wxs = [_expand_heads(wsc, expp_ref[d]) for d in range(2)]
        s_olds, s_upds, gls = [], [], []
        for d in range(2):
            kk, lg = hg[d]
            gl = jnp.sum(lg, axis=0, keepdims=True)
            kd = kk * jnp.exp(gl - bcs[d])
            s_old = s_scr[q, d]
            ss[d][q] = s_old.astype(BF16)
            s_olds.append(s_old)
            gls.append(gl)
            s_upds.append(_dot(gi[d][q].T.astype(BF16), kd.astype(BF16)))
        yield
        news = []
        for d in range(2):
            c_old = c_scr[q, d]
            n_old = n_scr[q, d]
            cs[d][q] = c_old.astype(BF16)
            ns[d][q] = n_old
            scx = wxs[d][CHUNK:CHUNK + 1]
            kw = mk[d][q] * (HEAD_D ** -0.5) * wxs[d][:CHUNK]
            c_upd = _dot(kw.T.astype(BF16), mv[d][q].astype(BF16))
            n_new = n_old * scx + jnp.sum(kw, axis=0, keepdims=True)
            news.append((c_old, scx, c_upd, n_new))
        yield
        for d in range(2):
            c_old, scx, c_upd, n_new = news[d]
            c_new = c_old * scx + diag_blocks(c_upd)
            s_new = s_olds[d] * jnp.exp(gls[d]) + diag_blocks(s_upds[d])
            c_scr[q, d] = c_new
            n_scr[q, d] = n_new
            s_scr[q, d] = s_new
            if emit_final:
                @pl.when(pos == nc - 1)
                def _():
                    nfin_ref[q, d] = n_new
                    for h in range(N_HEADS):
                        hs = slice(h * HEAD_D, (h + 1) * HEAD_D)
                        cfin_ref[q, d, h] = c_new[:, hs]
                        sfin_ref[q, d, h] = s_new[:, hs].T
        if emit_final:
            @pl.when(pos == nc - 1)
            def _():
                mfin_ref[q] = g + mm

    _round_robin([seq_step(q) for q in range(group)])


def _state_scan(proj, lb, consts, init, *, n_seq, nc, group, emit_final, layer=0):
    zero_init = init is None
    p4 = proj.reshape(n_seq, nc, CHUNK, PROJ_W)

    def fwd(p):
        return p

    def bwd(p):
        return nc - 1 - p

    def blk(col, w, idx):
        return pl.BlockSpec((group, None, CHUNK, w), lambda s, p: (s, idx(p), 0, col))

    in_specs, args = [], []
    for col, w in ((BLK_MK, REC_W), (BLK_MV, REC_W)):
        for idx in (fwd, bwd):
            in_specs.append(blk(col, w, idx))
            args.append(p4)
    in_specs += [blk(BLK_GFF, REC_W, fwd), blk(BLK_GFB, REC_W, bwd)]
    args += [p4, p4]
    for col, w in ((BLK_GI, REC_W), (BLK_GATE_I, LANE), (BLK_GATE_F, LANE)):
        for idx in (fwd, bwd):
            in_specs.append(blk(col, w, idx))
            args.append(p4)
    in_specs += [
        pl.BlockSpec((1, REC_W), lambda s, p: (0, 0)),
        pl.BlockSpec((2, CHUNK, 3 * CHUNK), lambda s, p: (0, 0, 0)),
        pl.BlockSpec((2, LANE, REC_W), lambda s, p: (0, 0, 0)),
    ]
    args += [lb, consts["cum3"], consts["expp"]]
    state_shapes = ((2, HEAD_D, REC_W), (2, 1, REC_W), (1, LANE), (2, HEAD_D, REC_W))
    if not zero_init:
        for shape in state_shapes:
            in_specs.append(pl.BlockSpec((group, None) + shape, lambda s, p, k=len(shape): (s, layer) + (0,) * k))
        args += list(init)

    out_specs, out_shape = [], []

    def add_out(shape, dtype):
        for idx in (fwd, bwd):
            out_specs.append(pl.BlockSpec((group, None) + shape, lambda s, p, idx=idx: (s, idx(p)) + (0,) * len(shape)))
            out_shape.append(jax.ShapeDtypeStruct((n_seq, nc) + shape, dtype))

    add_out((HEAD_D, REC_W), BF16)
    add_out((1, REC_W), F32)
    add_out((1, LANE), F32)
    add_out((HEAD_D, REC_W), BF16)
    if emit_final:
        per_head = (2, N_HEADS, HEAD_D, HEAD_D)
        for shape in (per_head, state_shapes[1], state_shapes[2], per_head):
            out_specs.append(pl.BlockSpec((group,) + shape, lambda s, p, k=len(shape): (s,) + (0,) * k))
            out_shape.append(jax.ShapeDtypeStruct((n_seq,) + shape, F32))

    outs = pl.pallas_call(
        functools.partial(_scan_kernel, nc=nc, group=group, zero_init=zero_init, emit_final=emit_final),
        grid=(n_seq // group, nc),
        in_specs=in_specs,
        out_specs=out_specs,
        out_shape=out_shape,
        scratch_shapes=[pltpu.VMEM((group,) + shape, F32) for shape in state_shapes],
        compiler_params=_params("parallel", "arbitrary"),
        name="state_scan",
    )(*args)
    return list(outs)


def _round_robin(gens, skew=0):
    results = [None] * len(gens)
    done = [False] * len(gens)
    rnd = 0
    while not all(done):
        for i, g in enumerate(gens):
            if done[i] or rnd < skew * i:
                continue
            try:
                next(g)
            except StopIteration as e:
                results[i] = e.value
                done[i] = True
        rnd += 1
    return results


def _mlstm_chunk(mq, mk, mv, mo, gti, gtf, csf, csb, nsf, nsb, msf, msb,
                 mnorm, cum3_ref, expp_ref, bd, bd2, itile, caus_ref):
    inv_d = 1.0 / HEAD_D
    is_f, ig, lf, b = _gate_terms(gti, gtf, gti, gtf, cum3_ref)
    yield
    r = ig - b
    row = lax.broadcasted_iota(jnp.int32, (CHUNK, LANE), 0)
    cm_f = r
    cm_b = r
    sh = 1
    while sh < CHUNK:
        cm_f = jnp.maximum(cm_f, jnp.where(row >= sh, pltpu.roll(cm_f, sh, 0), NEG_BIG))
        cm_b = jnp.maximum(cm_b, jnp.where(row < CHUNK - sh, pltpu.roll(cm_b, CHUNK - sh, 0), NEG_BIG))
        sh *= 2
    m_row = jnp.where(is_f[0:1], msf, msb)
    big_m = jnp.maximum(m_row, jnp.where(is_f, cm_f, cm_b))
    ws = jnp.exp(m_row - big_m)
    stacked = jnp.concatenate([big_m, ws, b + big_m, r], axis=0)

    qb = mq.astype(BF16)
    k_bd = _block_diag((mk * (HEAD_D ** -0.5)).astype(BF16), bd)
    v_bd = _block_diag(mv.astype(BF16), bd)
    s = _dot_t(qb, k_bd)
    ones3 = jnp.ones((CHUNK, 3 * CHUNK), BF16)
    ex = [_expand_heads(stacked, expp_ref[d]) for d in range(2)]
    yield
    r_bcast = [_sel_rows(ones3, ex[d][3 * CHUNK:] * itile) for d in range(2)]
    yield
    p = []
    for d in range(2):
        arg = jnp.where(caus_ref[d] > 0.0, r_bcast[d] - ex[d][:CHUNK], NEG_BIG)
        p.append(s * jnp.exp(arg))
    pv = _dot(jnp.concatenate(p, axis=0).astype(BF16), v_bd)
    qc = [_dot(qb, _block_diag(c_st, bd)) for c_st in (csf, csb)]
    all_sums = _seg_sum(jnp.concatenate([mq * nsf, p[0], mq * nsb, p[1]], axis=0), bd2)
    sums = [(all_sums[2 * d * CHUNK:(2 * d + 1) * CHUNK], all_sums[(2 * d + 1) * CHUNK:(2 * d + 2) * CHUNK])
            for d in range(2)]
    yield
    hsum = None
    for d in range(2):
        wsx = ex[d][CHUNK:2 * CHUNK]
        num = wsx * qc[d] + pv[d * CHUNK:(d + 1) * CHUNK]
        den = wsx * sums[d][0] + sums[d][1]
        hd = num / jnp.maximum(jnp.abs(den), jnp.exp(-ex[d][2 * CHUNK:3 * CHUNK]))
        hsum = hd if hsum is None else hsum + hd
    mu = _seg_sum(hsum, bd2) * inv_d
    yield
    hc = hsum - mu
    var = _seg_sum(hc * hc, bd2) * inv_d
    yield
    return hc * lax.rsqrt(var + EPS) * mnorm * jax.nn.sigmoid(mo)


def _hgrn_chunk(gq, gff, gfb, gi, gg, ssf, ssb, lb, hnorm, bd, bd2, mat3_ref, msk_ref):
    inv_d = 1.0 / HEAD_D
    qh = _silu(gq)
    gv_bd = _block_diag(gi.astype(BF16), bd)
    gates = [_hgrn_gates(fr, lb) for fr in (gff, gfb)]
    args = [_sel_rows(mat3_ref[d], gates[d][1]) for d in range(2)]
    yield
    a_tot = None
    inter = None
    for d, s_st in enumerate((ssf, ssb)):
        kk = gates[d][0]
        e = jnp.exp(args[d])

        def eb(i):
            return e[i * CHUNK:(i + 1) * CHUNK]

        t_in = _dot_t((qh * eb(0)).astype(BF16), _block_diag(s_st, bd))
        inter = t_in if inter is None else inter + t_in
        lg = gates[d][1]
        row = lax.broadcasted_iota(jnp.int32, lg.shape, 0)
        pos = (row if d == 0 else CHUNK - 1 - row) & 3
        back = (lambda x, n: pltpu.roll(x, n, 0)) if d == 0 else (lambda x, n: pltpu.roll(x, CHUNK - n, 0))
        d2 = lg + back(lg, 1)
        d3 = d2 + back(lg, 2)
        inner = [jnp.exp(jnp.where(pos - j == 1, lg, jnp.where(pos - j == 2, d2, jnp.where(pos - j == 3, d3, 0.0))))
                 for j in range(3)]
        stages = (
            (kk * eb(1), [qh * eb(2), qh * eb(3), qh * eb(4)]),
            (kk * eb(5), [qh * eb(6), qh * eb(7), qh * eb(8)]),
            (kk, [qh * inner[0], qh * inner[1], qh * inner[2], qh]),
        )
        outs = [_dot_t(jnp.concatenate(q_list, axis=0).astype(BF16), _block_diag(k_st.astype(BF16), bd))
                for k_st, q_list in stages]
        yield
        mi = 0
        for out, (_, q_list) in zip(outs, stages):
            for j in range(len(q_list)):
                term = msk_ref[d, mi] * out[j * CHUNK:(j + 1) * CHUNK]
                a_tot = term if a_tot is None else a_tot + term
                mi += 1
    o = _dot(a_tot.astype(BF16), gv_bd) + inter
    yield
    ms = _seg_sum(o * o, bd2) * inv_d
    yield
    return o * lax.rsqrt(ms + EPS) * hnorm * _silu(gg)


def _mixer_kernel(*refs, chunks):
    mq, mk, mv, mo, gq, gff, gfb, gi, gg, gti, gtf = refs[:11]
    csf, csb, nsf, nsb, msf, msb, ssf, ssb = refs[11:19]
    lb_ref, mnorm_ref, hnorm_ref, cum3_ref, expp_ref, bd_ref, bd2_ref, itile_ref, caus_ref, mat3_ref, msk_ref = refs[19:30]
    o_ref = refs[30]
    bd, bd2, itile = bd_ref[...], bd2_ref[...], itile_ref[...]
    lb, mnorm, hnorm = lb_ref[...], mnorm_ref[...], hnorm_ref[...]
    gens = []
    for c in range(chunks):
        rows = slice(c * CHUNK, (c + 1) * CHUNK)
        gens.append(_mlstm_chunk(*[r[rows, :] for r in (mq, mk, mv, mo, gti, gtf)],
                                 *[r[c] for r in (csf, csb, nsf, nsb, msf, msb)],
                                 mnorm, cum3_ref, expp_ref, bd, bd2, itile, caus_ref))
        gens.append(_hgrn_chunk(*[r[rows, :] for r in (gq, gff, gfb, gi, gg)], ssf[c], ssb[c],
                                lb, hnorm, bd, bd2, mat3_ref, msk_ref))
    outs = _round_robin(gens)
    for c in range(chunks):
        rows = slice(c * CHUNK, (c + 1) * CHUNK)
        o_ref[rows, :REC_W] = outs[2 * c].astype(BF16)
        o_ref[rows, REC_W:] = outs[2 * c + 1].astype(BF16)


def _mixer_outputs(proj, states, lb, mnorm, hnorm, consts, *, chunks):
    n = proj.shape[0]
    rows = chunks * CHUNK

    def blk(col, w):
        return pl.BlockSpec((rows, w), lambda c: (c, col))

    in_specs = [blk(c, REC_W) for c in (BLK_MQ, BLK_MK, BLK_MV, BLK_MO, BLK_GQ, BLK_GFF, BLK_GFB, BLK_GI, BLK_GG)]
    in_specs += [blk(BLK_GATE_I, LANE), blk(BLK_GATE_F, LANE)]
    args = [proj] * 11
    for arr in states:
        shape = arr.shape[2:]
        per_seq = arr.shape[1] // chunks
        in_specs.append(pl.BlockSpec((None, chunks) + shape,
                                     lambda c, k=len(shape), per_seq=per_seq: (c // per_seq, c % per_seq) + (0,) * k))
        args.append(arr)

    def whole(arr):
        return pl.BlockSpec(arr.shape, lambda c, k=arr.ndim: (0,) * k)

    for arr in (lb, mnorm, hnorm, consts["cum3"], consts["expp"], consts["bd"], consts["bd2"], consts["itile"],
                consts["caus"], consts["mat3"], consts["msk"]):
        in_specs.append(whole(arr))
        args.append(arr)
    return pl.pallas_call(
        functools.partial(_mixer_kernel, chunks=chunks),
        grid=(n // rows,),
        in_specs=in_specs,
        out_specs=pl.BlockSpec((rows, 2 * REC_W), lambda c: (c, 0)),
        out_shape=jax.ShapeDtypeStruct((n, 2 * REC_W), BF16),
        compiler_params=_params("parallel"),
        name="mixer_outputs",
    )(*args)


def _rot_perm():
    q = MLA_DR // 4
    idx = np.concatenate([np.arange(q, 2 * q), np.arange(0, q), np.arange(3 * q, 4 * q), np.arange(2 * q, 3 * q)])
    sign = np.concatenate([-np.ones(q), np.ones(q), -np.ones(q), np.ones(q)]).astype(np.float32)
    return idx, sign


def _pad_cols(a, width):
    return jnp.pad(a, [(0, 0)] * (a.ndim - 1) + [(0, width - a.shape[-1])])


def _layout_in_proj(w_in, b_in):
    off = np.concatenate([[0], np.cumsum(IN_SIZES)])
    idx, sign = _rot_perm()

    def lay(a):
        def cols(i, j=None):
            return a[..., int(off[i]):int(off[(i if j is None else j) + 1])]

        kpe = cols(2)
        return jnp.concatenate([
            cols(0), cols(1),
            _pad_cols(jnp.concatenate([kpe, kpe[..., idx] * sign], axis=-1), LANE),
            cols(3, 6), cols(11, 15),
            _pad_cols(jnp.concatenate([cols(7), cols(8)], axis=-1), LANE),
            _pad_cols(jnp.concatenate([cols(9), cols(10)], axis=-1), LANE),
        ], axis=-1)

    return lay(w_in).astype(BF16), lay(b_in)[:, None, :]


def _layout_query_weights(w_uq, w_abs):
    idx, sign = _rot_perm()
    w = w_uq.reshape(DEPTH, Q_RANK, MLA_H, MLA_DN + MLA_DR)
    pe = w[..., MLA_DN:]
    slot = jnp.concatenate([jnp.transpose(w_abs, (0, 2, 1, 3)), pe, pe[..., idx] * sign], axis=-1)
    return _pad_cols(slot, Q_SLOT).reshape(DEPTH, Q_RANK, MLA_H * Q_SLOT).astype(BF16)


def _layout_value_weights(w_ukv):
    w = w_ukv.reshape(DEPTH, KV_RANK, MLA_H, MLA_DN + MLA_DV)[..., MLA_DN:]
    eye = jnp.eye(MLA_H, dtype=w.dtype)
    bdw = jnp.einsum("lrhe,hg->lhrge", w, eye)
    return bdw.reshape(DEPTH, MLA_H * KV_RANK, MLA_H * MLA_DV).astype(BF16)


def _rope_tables(n):
    n_rows = n // GRID_W
    row = jnp.repeat(jnp.arange(n_rows, dtype=F32), GRID_W)
    col = jnp.tile(jnp.arange(GRID_W, dtype=F32), n_rows)
    half = MLA_DR // 2
    freqs = ROPE_BASE ** (-jnp.arange(half // 2, dtype=F32) * (2.0 / half))
    ar = row[:, None] * freqs
    ac = col[:, None] * freqs
    ang = jnp.concatenate([ar, ar, ac, ac], -1)
    return jnp.cos(ang), jnp.sin(ang)


def kernel(x_prompt, x_sample, cache_mla_ckv, cache_mla_kpe, state_mlstm_C, state_mlstm_n, state_mlstm_m, state_hgrn_S, c, c_ctx, w_mod, b_mod, w_in, b_in, mla_q_norm, w_uq, mla_kv_norm, w_ukv, mlstm_norm, hgrn_lb_logits, hgrn_norm, w_out, ln1_g, ln1_b, w_ffn_in, w_ffn_out, ln2_g, ln2_b):
    B, T, D = x_prompt.shape
    Bs, Ts, _ = x_sample.shape
    past = cache_mla_ckv.shape[2]
    consts = _recurrent_constants()

    cvec = jnp.concatenate([c_ctx[None, :], c, jnp.zeros((8 - 1 - Bs, D), F32)], axis=0)
    mod = jnp.transpose(_modulation(cvec, w_mod, b_mod), (0, 2, 1, 3))
    lbs = _hgrn_lower_bounds(hgrn_lb_logits)

    w_in_p, b_in_p = _layout_in_proj(w_in, b_in)
    wq4 = w_uq.reshape(DEPTH, Q_RANK, MLA_H, MLA_DN + MLA_DR)
    wkv4 = w_ukv.reshape(DEPTH, KV_RANK, MLA_H, MLA_DN + MLA_DV)
    w_abs = _absorbed_query_weights(jnp.transpose(wq4[..., :MLA_DN], (0, 2, 1, 3)),
                                    jnp.transpose(wkv4[..., :MLA_DN], (0, 2, 1, 3)))
    wq_p = _layout_query_weights(w_uq, w_abs)
    wuv_p = _layout_value_weights(w_ukv)
    w_out_b = w_out.astype(BF16)
    w_ffn_in_b = w_ffn_in.astype(BF16)
    w_ffn_out_b = w_ffn_out.astype(BF16)

    scale = (MLA_DN + MLA_DR) ** -0.5 * float(np.log2(np.e))
    cos, sin = _rope_tables(Ts)
    qtab_s = scale * jnp.concatenate([jnp.ones((Ts, KV_RANK), F32), cos, sin, jnp.zeros((Ts, Q_SLOT - KV_RANK - 2 * MLA_DR), F32)], axis=1)
    ktab_s = jnp.concatenate([cos, sin, jnp.zeros((Ts, LANE - 2 * MLA_DR), F32)], axis=1)
    TM_C = 256
    lane_q = np.arange(Q_SLOT)
    qtab_c = jnp.asarray(np.tile(scale * (lane_q < KV_RANK + MLA_DR), (TM_C, 1)), F32)
    ktab_c = jnp.asarray(np.tile(np.arange(LANE) < MLA_DR, (TM_C, 1)), F32)

    kcache = jnp.concatenate([cache_mla_ckv, cache_mla_kpe, cache_mla_kpe,
                              jnp.ones(cache_mla_kpe.shape[:-1] + (1,), F32),
                              jnp.zeros(cache_mla_kpe.shape[:-1] + (Q_SLOT - ONE_LANE - 1,), F32)], axis=-1).astype(BF16)
    c0 = jnp.transpose(state_mlstm_C, (0, 1, 2, 4, 3, 5)).reshape(Bs, DEPTH, 2, HEAD_D, REC_W)
    n0 = state_mlstm_n.reshape(Bs, DEPTH, 2, 1, REC_W)
    m0 = _pad_cols(state_mlstm_m.reshape(Bs, DEPTH, 1, 2 * N_HEADS), LANE)
    s0 = jnp.transpose(state_hgrn_S, (0, 1, 2, 5, 3, 4)).reshape(Bs, DEPTH, 2, HEAD_D, REC_W)

    xp = x_prompt.reshape(B * T, D)
    xs = x_sample.reshape(Bs * Ts, D)
    nc_c, nc_s = T // CHUNK, Ts // CHUNK
    TM = 512
    TM_F = 1024
    ckv_l, kpe_l, fin_l = [], [], []
    for l in range(DEPTH):
        lb = lbs[l][None, :]
        mnorm, hnorm = mlstm_norm[l][None, :], hgrn_norm[l][None, :]
        qg, kg = mla_q_norm[l][None, :], mla_kv_norm[l][None, :]
        g1, b1, g2, b2 = ln1_g[l][None, :], ln1_b[l][None, :], ln2_g[l][None, :], ln2_b[l][None, :]
        mod_c, mod_s = mod[l, 0:1], mod[l, 1:1 + Bs]

        proj = _inproj(xp, mod_c, w_in_p, b_in_p, layer=l, tm=TM, tiles_per_mod=B * T // TM)
        qa, ka, ckvn = _attn_prep(proj, qtab_c, ktab_c, wq_p, qg, kg, layer=l, tm=TM_C, tab_tiles=1)
        o_lat = _attention(qa, ka, None, n_seq=B, seq_len=T, tq=T, heads=MLA_H, lockstep=MLA_H)
        scan = _state_scan(proj, lb, consts, None, n_seq=B, nc=nc_c, group=8, emit_final=True)
        mg = _mixer_outputs(proj, scan[:8], lb, mnorm, hnorm, consts, chunks=4)
        x1 = _outproj(xp, o_lat, mg, mod_c, wuv_p, w_out_b, g1, b1, layer=l, tm=TM_F, tiles_per_mod=B * T // TM_F)
        xp = _ffn(x1, mod_c, w_ffn_in_b, w_ffn_out_b, g2, b2, layer=l, tm=TM_F, tiles_per_mod=B * T // TM_F)
        ckv_l.append(ckvn.reshape(B, T, KV_RANK))
        kpe_l.append(proj[:, 3 * LANE:3 * LANE + MLA_DR].reshape(B, T, MLA_DR))
        fin_l.append(scan[8:])

        proj = _inproj(xs, mod_s, w_in_p, b_in_p, layer=l, tm=TM, tiles_per_mod=Ts // TM)
        qa, ka, _ = _attn_prep(proj, qtab_s, ktab_s, wq_p, qg, kg, layer=l, tm=TM, tab_tiles=Ts // TM)
        o_lat = _attention(qa, ka, kcache, n_seq=Bs, seq_len=Ts, tq=256, heads=8, lockstep=1, layer=l)
        scan = _state_scan(proj, lb, consts, (c0, n0, m0, s0), n_seq=Bs, nc=nc_s, group=Bs, emit_final=False, layer=l)
        mg = _mixer_outputs(proj, scan[:8], lb, mnorm, hnorm, consts, chunks=8)
        x1 = _outproj(xs, o_lat, mg, mod_s, wuv_p, w_out_b, g1, b1, layer=l, tm=TM_F, tiles_per_mod=Ts // TM_F)
        xs = _ffn(x1, mod_s, w_ffn_in_b, w_ffn_out_b, g2, b2, layer=l, tm=TM_F, tiles_per_mod=Ts // TM_F)

    new_ckv = jnp.stack(ckv_l, axis=1)
    new_kpe = jnp.stack(kpe_l, axis=1)
    new_c = jnp.stack([f[0] for f in fin_l], axis=1)
    new_n = jnp.stack([f[1].reshape(B, 2, N_HEADS, HEAD_D) for f in fin_l], axis=1)
    new_m = jnp.stack([f[2][:, 0, :2 * N_HEADS].reshape(B, 2, N_HEADS) for f in fin_l], axis=1)
    new_s = jnp.stack([f[3] for f in fin_l], axis=1)
    return (xp.reshape(B, T, D), xs.reshape(Bs, Ts, D), new_ckv, new_kpe, new_c, new_n, new_m, new_s)
```
